```python
import math
import jax, jax.numpy as jnp
from jax import lax
import numpy as np

D_MODEL = 2048
BATCH = 2
SEQ = 4096
DEPTH = 1
DEC_BATCH = 32
DEC_SEQ = 4
PAST_LEN = 8192
PAGE_SIZE = 128

HEAD_DIM = 128
N_HEADS = D_MODEL // HEAD_DIM
N_HEADS_SB = N_HEADS // 2
N_HEADS_MB = N_HEADS - N_HEADS_SB
D_SB = N_HEADS_SB * HEAD_DIM
D_MB = N_HEADS_MB * HEAD_DIM
D_IN = 3 * (D_SB + D_MB)
MOBA_BLOCK = 256
MOBA_TOPK = 3
ROPE_THETA = 10000.0
D_FF = 5632
CONV_WIDTH = 3
Q_BLOCK = 128
LN_EPS = 1e-5
RMS_EPS = 1e-6
DN_ALPHA = (2 * DEPTH) ** 0.25
DN_BETA = (8 * DEPTH) ** -0.25
NEG = -1e30

kernel_name = 'hybrid_stickbreak_moba_convffn_step'


def layer_norm(x, g, b):
    xf = x.astype(jnp.float32)
    mu = xf.mean(-1, keepdims=True)
    var = jnp.square(xf - mu).mean(-1, keepdims=True)
    y = (xf - mu) * lax.rsqrt(var + LN_EPS) * g.astype(jnp.float32) + b.astype(jnp.float32)
    return y.astype(x.dtype)


def head_rms_norm(o, g):
    ms = jnp.mean(jnp.square(o), axis=-1, keepdims=True)
    return o * lax.rsqrt(ms + RMS_EPS) * g.astype(jnp.float32).reshape(-1, HEAD_DIM)


def rope(x, pos):
    half = HEAD_DIM // 2
    inv = ROPE_THETA ** (-jnp.arange(half, dtype=jnp.float32) / half)
    ang = pos.astype(jnp.float32)[:, None] * inv[None, :]
    cos = jnp.cos(ang)[:, None, :]
    sin = jnp.sin(ang)[:, None, :]
    xf = x.astype(jnp.float32)
    x1, x2 = xf[..., :half], xf[..., half:]
    return jnp.concatenate([x1 * cos - x2 * sin, x2 * cos + x1 * sin], axis=-1).astype(x.dtype)


def project_qkv(h, pos, w_in):
    b, t, _ = h.shape
    proj = jnp.einsum('btd,de->bte', h, w_in)
    cuts = [D_SB, 2 * D_SB, 3 * D_SB, 3 * D_SB + D_MB, 3 * D_SB + 2 * D_MB]
    qa, ka, va, qb, kb, vb = jnp.split(proj, cuts, axis=-1)
    qa, ka, va = (a.reshape(b, t, N_HEADS_SB, HEAD_DIM) for a in (qa, ka, va))
    qb, kb, vb = (a.reshape(b, t, N_HEADS_MB, HEAD_DIM) for a in (qb, kb, vb))
    return qa, ka, va, rope(qb, pos), rope(kb, pos), vb


def stick_breaking(q, k, v, q_pos, k_pos):
    z = jnp.einsum('bqhd,bkhd->bhqk', q.astype(jnp.float32), k.astype(jnp.float32)) / math.sqrt(HEAD_DIM)
    valid = k_pos[None, :] < q_pos[:, None]
    log_1mb = jnp.where(valid, jax.nn.log_sigmoid(-z), 0.0)
    after = lax.cumsum(log_1mb, axis=3, reverse=True) - log_1mb
    w = jnp.where(valid, jnp.exp(jax.nn.log_sigmoid(z) + after), 0.0)
    return jnp.einsum('bhqk,bkhd->bqhd', w, v.astype(jnp.float32))


def moba_blocks(k, v):
    b, l, h, d = k.shape
    nb = -(-l // MOBA_BLOCK)
    pad = nb * MOBA_BLOCK - l

    def blk(a):
        a = jnp.pad(a, ((0, 0), (0, pad), (0, 0), (0, 0)))
        return a.reshape(b, nb, MOBA_BLOCK, h, d).transpose(0, 3, 1, 2, 4)

    kb, vb = blk(k), blk(v)
    return kb, vb, kb.astype(jnp.float32).mean(axis=3)


def moba_attend(q, q_pos, kb, vb, k_mean):
    b, tq, h, d = q.shape
    nb = kb.shape[2]
    qf = q.astype(jnp.float32)
    own = q_pos // MOBA_BLOCK
    gate = jnp.einsum('bqhd,bhnd->bhqn', qf, k_mean)
    past = jnp.arange(nb)[None, :] < own[:, None]
    gate = jnp.where(past, gate, NEG)
    n_sel = min(MOBA_TOPK, nb)
    _, top = lax.top_k(gate, n_sel)
    top = top.astype(jnp.int32)
    sel_ok = top < own[:, None]
    own_b = jnp.broadcast_to(own[:, None].astype(jnp.int32), (b, h, tq, 1))
    idx = jnp.concatenate([top, own_b], axis=-1)
    slot_ok = jnp.concatenate([sel_ok, jnp.ones((b, h, tq, 1), dtype=bool)], axis=-1)
    bi = jnp.arange(b)[:, None, None, None]
    hi = jnp.arange(h)[None, :, None, None]
    kg = kb[bi, hi, idx].astype(jnp.float32)
    vg = vb[bi, hi, idx].astype(jnp.float32)
    logits = jnp.einsum('bqhd,bhqjmd->bhqjm', qf, kg) / math.sqrt(HEAD_DIM)
    k_pos = idx[..., None] * MOBA_BLOCK + jnp.arange(MOBA_BLOCK)
    valid = slot_ok[..., None] & (k_pos <= q_pos[:, None, None])
    logits = jnp.where(valid, logits, NEG)
    j = n_sel + 1
    p = jax.nn.softmax(logits.reshape(b, h, tq, j * MOBA_BLOCK), axis=-1).reshape(b, h, tq, j, MOBA_BLOCK)
    return jnp.einsum('bhqjm,bhqjmd->bqhd', p, vg)


def sweep(fn, q, q_pos):
    b, t, h, d = q.shape
    n = t // Q_BLOCK
    qb = q.reshape(b, n, Q_BLOCK, h, d).swapaxes(0, 1)
    pb = q_pos.reshape(n, Q_BLOCK)
    out = lax.map(lambda a: fn(a[0], a[1]), (qb, pb))
    return out.swapaxes(0, 1).reshape(b, t, h, d)


def merge_heads(o_sb, o_mb, g_sb, g_mb, w_out, dtype):
    o = jnp.concatenate([head_rms_norm(o_sb, g_sb), head_rms_norm(o_mb, g_mb)], axis=2)
    b, t = o.shape[:2]
    return jnp.einsum('bte,ed->btd', o.reshape(b, t, D_MODEL).astype(dtype), w_out)


def conv_ffn(h, conv_prev, w_gate, w_up, conv_w, conv_b, w_down):
    t = h.shape[1]
    g = jnp.einsum('btd,df->btf', h, w_gate)
    u = jnp.einsum('btd,df->btf', h, w_up)
    gp = jnp.concatenate([conv_prev.astype(g.dtype), g], axis=1)
    gpf = gp.astype(jnp.float32)
    c = conv_b.astype(jnp.float32)
    for i in range(CONV_WIDTH):
        c = c + conv_w[i].astype(jnp.float32) * gpf[:, i:i + t]
    a = jax.nn.gelu(c, approximate=False) * u.astype(jnp.float32)
    y = jnp.einsum('btf,fd->btd', a.astype(h.dtype), w_down)
    return y, gp[:, gp.shape[1] - (CONV_WIDTH - 1):]


def gather_pages(pool, page_table):
    g = pool[page_table]
    return g.reshape(page_table.shape[0], -1, *pool.shape[2:])


def setup_inputs(seed: int = 0) -> dict:
    key = jax.random.key(seed)
    ks = jax.random.split(key, 24)
    n_pages = PAST_LEN // PAGE_SIZE
    n_used = DEC_BATCH * n_pages
    n_phys = n_used + n_used // 4

    def nrm(k, shape, scale):
        return jax.random.normal(k, shape, jnp.float32) * scale

    col_scale = jnp.concatenate([
        jnp.ones((2 * D_SB,), jnp.float32), jnp.full((D_SB,), DN_BETA, jnp.float32),
        jnp.ones((2 * D_MB,), jnp.float32), jnp.full((D_MB,), DN_BETA, jnp.float32)])
    return {
        'x_prompt': nrm(ks[0], (BATCH, SEQ, D_MODEL), 1.0),
        'x_sample': nrm(ks[1], (DEC_BATCH, DEC_SEQ, D_MODEL), 1.0),
        'cache_kv_sb': nrm(ks[2], (DEPTH, n_phys, PAGE_SIZE, 2, N_HEADS_SB, HEAD_DIM), 1.0),
        'cache_kv_mb': nrm(ks[3], (DEPTH, n_phys, PAGE_SIZE, 2, N_HEADS_MB, HEAD_DIM), 1.0),
        'state_conv': nrm(ks[4], (DEPTH, DEC_BATCH, CONV_WIDTH - 1, D_FF), DN_BETA),
        'page_table': jax.random.permutation(ks[5], n_phys)[:n_used].reshape(DEC_BATCH, n_pages).astype(jnp.int32),
        'ln_in_g': 1.0 + nrm(ks[6], (D_MODEL,), 0.02),
        'ln_in_b': nrm(ks[7], (D_MODEL,), 0.02),
        'w_in': nrm(ks[8], (DEPTH, D_MODEL, D_IN), D_MODEL ** -0.5) * col_scale,
        'g_sb': 1.0 + nrm(ks[9], (DEPTH, D_SB), 0.02),
        'g_mb': 1.0 + nrm(ks[10], (DEPTH, D_MB), 0.02),
        'w_out': nrm(ks[11], (DEPTH, D_MODEL, D_MODEL), DN_BETA * D_MODEL ** -0.5),
        'ln1_g': 1.0 + nrm(ks[12], (DEPTH, D_MODEL), 0.02),
        'ln1_b': nrm(ks[13], (DEPTH, D_MODEL), 0.02),
        'w_gate': nrm(ks[14], (DEPTH, D_MODEL, D_FF), DN_BETA * D_MODEL ** -0.5),
        'w_up': nrm(ks[15], (DEPTH, D_MODEL, D_FF), DN_BETA * D_MODEL ** -0.5),
        'conv_w': nrm(ks[16], (DEPTH, CONV_WIDTH, D_FF), CONV_WIDTH ** -0.5),
        'conv_b': nrm(ks[17], (DEPTH, D_FF), 0.01),
        'w_down': nrm(ks[18], (DEPTH, D_FF, D_MODEL), DN_BETA * D_FF ** -0.5),
        'ln2_g': 1.0 + nrm(ks[19], (DEPTH, D_MODEL), 0.02),
        'ln2_b': nrm(ks[20], (DEPTH, D_MODEL), 0.02),
    }


def reference(x_prompt, x_sample, cache_kv_sb, cache_kv_mb, state_conv, page_table,
              ln_in_g, ln_in_b, w_in, g_sb, g_mb, w_out, ln1_g, ln1_b,
              w_gate, w_up, conv_w, conv_b, w_down, ln2_g, ln2_b):
    n_p, t_p = x_prompt.shape[:2]
    t_s = x_sample.shape[1]
    past = page_table.shape[1] * PAGE_SIZE
    pos_p = jnp.arange(t_p, dtype=jnp.int32)
    pos_s = past + jnp.arange(t_s, dtype=jnp.int32)
    pos_all_s = jnp.arange(past + t_s, dtype=jnp.int32)
    conv0 = jnp.zeros((n_p, CONV_WIDTH - 1, D_FF), x_prompt.dtype)

    hp = layer_norm(x_prompt, ln_in_g, ln_in_b)
    hs = layer_norm(x_sample, ln_in_g, ln_in_b)
    kv_sb_p, kv_mb_p, conv_p_l, kv_sb_s, kv_mb_s, conv_s_l = [], [], [], [], [], []
    for l in range(DEPTH):
        qa, ka, va, qb, kb, vb = project_qkv(hp, pos_p, w_in[l])
        o_sb = sweep(lambda qq, pp: stick_breaking(qq, ka, va, pp, pos_p), qa, pos_p)
        kbl, vbl, kmean = moba_blocks(kb, vb)
        o_mb = sweep(lambda qq, pp: moba_attend(qq, pp, kbl, vbl, kmean), qb, pos_p)
        mix = merge_heads(o_sb, o_mb, g_sb[l], g_mb[l], w_out[l], hp.dtype)
        hp = layer_norm(DN_ALPHA * hp + mix, ln1_g[l], ln1_b[l])
        ff, conv_p = conv_ffn(hp, conv0, w_gate[l], w_up[l], conv_w[l], conv_b[l], w_down[l])
        hp = layer_norm(DN_ALPHA * hp + ff, ln2_g[l], ln2_b[l])
        kv_sb_p.append(jnp.stack([ka, va], axis=2))
        kv_mb_p.append(jnp.stack([kb, vb], axis=2))
        conv_p_l.append(conv_p)

        qa, ka, va, qb, kb, vb = project_qkv(hs, pos_s, w_in[l])
        past_sb = gather_pages(cache_kv_sb[l], page_table)
        k_all = jnp.concatenate([past_sb[:, :, 0], ka], axis=1)
        v_all = jnp.concatenate([past_sb[:, :, 1], va], axis=1)
        o_sb = stick_breaking(qa, k_all, v_all, pos_s, pos_all_s)
        past_mb = gather_pages(cache_kv_mb[l], page_table)
        kbl, vbl, kmean = moba_blocks(jnp.concatenate([past_mb[:, :, 0], kb], axis=1),
                                      jnp.concatenate([past_mb[:, :, 1], vb], axis=1))
        o_mb = moba_attend(qb, pos_s, kbl, vbl, kmean)
        mix = merge_heads(o_sb, o_mb, g_sb[l], g_mb[l], w_out[l], hs.dtype)
        hs = layer_norm(DN_ALPHA * hs + mix, ln1_g[l], ln1_b[l])
        ff, conv_s = conv_ffn(hs, state_conv[l], w_gate[l], w_up[l], conv_w[l], conv_b[l], w_down[l])
        hs = layer_norm(DN_ALPHA * hs + ff, ln2_g[l], ln2_b[l])
        kv_sb_s.append(jnp.stack([ka, va], axis=2))
        kv_mb_s.append(jnp.stack([kb, vb], axis=2))
        conv_s_l.append(conv_s)

    return (hp, hs, jnp.stack(kv_sb_p), jnp.stack(kv_mb_p), jnp.stack(conv_p_l),
            jnp.stack(kv_sb_s), jnp.stack(kv_mb_s), jnp.stack(conv_s_l))
```

```python
import functools
import math

import jax
import jax.numpy as jnp
from jax import lax
from jax.experimental import pallas as pl
from jax.experimental.pallas import tpu as pltpu

F32 = jnp.float32
BF16 = jnp.bfloat16

HEAD_DIM = 128
N_HEADS_SB = 8
N_HEADS_MB = 8
D_SB = N_HEADS_SB * HEAD_DIM
D_MB = N_HEADS_MB * HEAD_DIM
MOBA_BLOCK = 256
MOBA_TOPK = 3
ROPE_THETA = 10000.0
CONV_WIDTH = 3
PAGE_SIZE = 128
LN_EPS = 1e-5
RMS_EPS = 1e-6
NEG = -1e30
SCALE = 1.0 / math.sqrt(HEAD_DIM)

SB_SKIP_BELOW = -105.0

SB_TQ = 128
LANES = 128
VMEM_LIMIT_V7X = 56 * 1024 * 1024


def _cparams(sem):
    return pltpu.CompilerParams(dimension_semantics=sem, vmem_limit_bytes=VMEM_LIMIT_V7X)


def _layer_norm(x, g, b):
    mu = jnp.mean(x, axis=-1, keepdims=True)
    xc = x - mu
    var = jnp.mean(xc * xc, axis=-1, keepdims=True)
    return xc * lax.rsqrt(var + LN_EPS) * g + b


def _head_rms(o, g):
    ms = jnp.mean(o * o, axis=-1, keepdims=True)
    return o * lax.rsqrt(ms + RMS_EPS) * g


def _dot_nt(a, b):
    return lax.dot_general(a, b, (((1,), (1,)), ((), ())), preferred_element_type=F32)


def _dot(a, b):
    return jnp.dot(a, b, preferred_element_type=F32)


def _split_bf16(x):
    hi = x.astype(BF16)
    lo = (x - hi.astype(F32)).astype(BF16)
    return hi, lo


def _softplus(z):
    return jnp.maximum(z, 0.0) + jnp.log1p(jnp.exp(-jnp.abs(z)))


def _tri(n):
    r = lax.broadcasted_iota(jnp.int32, (n, n), 0)
    s = lax.broadcasted_iota(jnp.int32, (n, n), 1)
    return (r > s).astype(BF16)


QKV_TN = 512
QKV_STEPS_PER_SEG = D_SB // QKV_TN

def _ln_qkv_kernel(x_ref, g_ref, b_ref, cos_ref, sin_ref, w_ref,
                   h_ref, qsb_ref, kvsb_ref, qmb_ref, kvmb_ref, hb_ref):
    j = pl.program_id(1)

    @pl.when(j == 0)
    def _():
        h = _layer_norm(x_ref[...], g_ref[...], b_ref[...])
        h_ref[...] = h
        hb_ref[...] = h.astype(BF16)

    seg = j // QKV_STEPS_PER_SEG
    acc = _dot(hb_ref[...], w_ref[...])

    def rope(a):
        cos = cos_ref[...]
        sin = sin_ref[...]
        heads = []
        for hh in range(a.shape[1] // HEAD_DIM):
            ah = a[:, hh * HEAD_DIM:(hh + 1) * HEAD_DIM]
            heads.append(ah * cos + pltpu.roll(ah, HEAD_DIM // 2, axis=1) * sin)
        return jnp.concatenate(heads, axis=1)

    @pl.when(seg == 0)
    def _():
        qsb_ref[...] = acc.astype(BF16)

    @pl.when((seg == 1) | (seg == 2))
    def _():
        kvsb_ref[...] = acc

    @pl.when(seg == 3)
    def _():
        qmb_ref[...] = rope(acc)

    @pl.when(seg == 4)
    def _():
        kvmb_ref[...] = rope(acc)

    @pl.when(seg == 5)
    def _():
        kvmb_ref[...] = acc


def _ln_qkv(x, g, b, cos, sin, w_bf16, tm, n_pos_tiles):
    rows, d = x.shape
    assert w_bf16.shape[1] == 3 * D_SB + 3 * D_MB and D_SB == D_MB
    sps = QKV_STEPS_PER_SEG
    grid = (rows // tm, 6 * sps)
    col = lambda j, first_seg, n_seg: jnp.minimum(jnp.maximum(j - first_seg * sps, 0), n_seg * sps - 1)
    return pl.pallas_call(
        _ln_qkv_kernel,
        grid=grid,
        in_specs=[
            pl.BlockSpec((tm, d), lambda i, j: (i, 0)),
            pl.BlockSpec((1, d), lambda i, j: (0, 0)),
            pl.BlockSpec((1, d), lambda i, j: (0, 0)),
            pl.BlockSpec((tm, HEAD_DIM), lambda i, j: (i % n_pos_tiles, 0)),
            pl.BlockSpec((tm, HEAD_DIM), lambda i, j: (i % n_pos_tiles, 0)),
            pl.BlockSpec((d, QKV_TN), lambda i, j: (0, j)),
        ],
        out_specs=[
            pl.BlockSpec((tm, d), lambda i, j: (i, 0)),
            pl.BlockSpec((tm, QKV_TN), lambda i, j: (i, col(j, 0, 1))),
            pl.BlockSpec((tm, QKV_TN), lambda i, j: (i, col(j, 1, 2))),
            pl.BlockSpec((tm, QKV_TN), lambda i, j: (i, col(j, 3, 1))),
            pl.BlockSpec((tm, QKV_TN), lambda i, j: (i, col(j, 4, 2))),
        ],
        out_shape=[
            jax.ShapeDtypeStruct((rows, d), F32),
            jax.ShapeDtypeStruct((rows, D_SB), BF16),
            jax.ShapeDtypeStruct((rows, 2 * D_SB), F32),
            jax.ShapeDtypeStruct((rows, D_MB), F32),
            jax.ShapeDtypeStruct((rows, 2 * D_MB), F32),
        ],
        scratch_shapes=[pltpu.VMEM((tm, d), BF16)],
        compiler_params=_cparams(("arbitrary", "arbitrary")),
        name="ln_qkv",
    )(x, g, b, cos, sin, w_bf16)


def _sb_block(q, k, v, c, acc, tri, valid):
    z = _dot_nt(q, k) * SCALE
    sp = _softplus(z)
    l1mb = -sp if valid is None else jnp.where(valid, -sp, 0.0)
    hi, lo = _split_bf16(l1mb)
    after = _dot(hi, tri) + _dot(lo, tri) + c
    w = jnp.exp(z - sp + after)
    if valid is not None:
        w = jnp.where(valid, w, 0.0)
    acc = acc + _dot(w.astype(BF16), v)
    c = c + jnp.sum(l1mb, axis=-1, keepdims=True)
    return c, acc


def _sb_prompt_kernel(q_ref, k_ref, v_ref, g_ref, o_ref, kb_ref, vb_ref):
    qi = pl.program_id(2)

    @pl.when(qi == 0)
    def _():
        kb_ref[...] = k_ref[...].astype(BF16)
        vb_ref[...] = v_ref[...].astype(BF16)

    q = q_ref[...]
    tri = _tri(SB_TQ)
    row = lax.broadcasted_iota(jnp.int32, (SB_TQ, SB_TQ), 0)
    col = lax.broadcasted_iota(jnp.int32, (SB_TQ, SB_TQ), 1)

    def kv_block(kb):
        start = pl.multiple_of(kb * SB_TQ, SB_TQ)
        return kb_ref[pl.ds(start, SB_TQ), :], vb_ref[pl.ds(start, SB_TQ), :]

    k0, v0 = kv_block(qi)
    c, acc = _sb_block(q, k0, v0, jnp.zeros((SB_TQ, 1), F32), jnp.zeros((SB_TQ, HEAD_DIM), F32),
                       tri, col < row)

    def cond(st):
        kb, c, _ = st
        return (kb >= 0) & (jnp.max(c) > SB_SKIP_BELOW)

    def body(st):
        kb, c, acc = st
        k, v = kv_block(kb)
        c, acc = _sb_block(q, k, v, c, acc, tri, None)
        return kb - 1, c, acc

    _, _, acc = lax.while_loop(cond, body, (qi - 1, c, acc))
    o_ref[...] = _head_rms(acc, g_ref[...]).astype(BF16)


def _sb_prompt(q, kv, g, n_batch, t):
    rows = q.shape[0]
    nq = t // SB_TQ
    return pl.pallas_call(
        _sb_prompt_kernel,
        grid=(n_batch, N_HEADS_SB, nq),
        in_specs=[
            pl.BlockSpec((SB_TQ, HEAD_DIM), lambda b, h, i: (b * nq + i, h)),
            pl.BlockSpec((t, HEAD_DIM), lambda b, h, i: (b, h)),
            pl.BlockSpec((t, HEAD_DIM), lambda b, h, i: (b, N_HEADS_SB + h)),
            pl.BlockSpec((1, HEAD_DIM), lambda b, h, i: (0, h)),
        ],
        out_specs=pl.BlockSpec((SB_TQ, HEAD_DIM), lambda b, h, i: (b * nq + i, h)),
        out_shape=jax.ShapeDtypeStruct((rows, D_SB), BF16),
        scratch_shapes=[pltpu.VMEM((t, HEAD_DIM), BF16), pltpu.VMEM((t, HEAD_DIM), BF16)],
        compiler_params=_cparams(("arbitrary", "arbitrary", "arbitrary")),
        name="sb_prompt",
    )(q, kv, kv, g)


def _topk_mask(gate, n_valid):
    col = lax.broadcasted_iota(jnp.int32, gate.shape, 1)
    colf = col.astype(F32)
    g = jnp.where(col < n_valid, gate, NEG)
    sel = jnp.zeros(gate.shape, F32)
    for _ in range(MOBA_TOPK):
        m = jnp.max(g, axis=-1, keepdims=True)
        first = jnp.min(jnp.where(g == m, colf, float(gate.shape[1])), axis=-1, keepdims=True)
        pick = (colf == first) & (m > 0.5 * NEG)
        sel = jnp.where(pick, 1.0, sel)
        g = jnp.where(pick, NEG, g)
    return sel


def _moba_prompt_kernel(q_ref, k_ref, v_ref, g_ref, o_ref, kb_ref, vb_ref, km_ref, *, nb):
    qi = pl.program_id(2)
    tq = MOBA_BLOCK

    @pl.when(qi == 0)
    def _():
        kb_ref[...] = k_ref[...].astype(BF16)
        vb_ref[...] = v_ref[...].astype(BF16)
        km_ref[...] = jnp.zeros(km_ref.shape, F32)
        for n in range(nb):
            km_ref[n:n + 1, :] = jnp.mean(k_ref[n * tq:(n + 1) * tq, :], axis=0, keepdims=True)

    qh = q_ref[...].astype(BF16)
    gate = _dot_nt(qh, km_ref[...].astype(BF16))
    sel = _topk_mask(gate, qi)
    bcol = lax.broadcasted_iota(jnp.int32, (tq, LANES), 1)

    row = lax.broadcasted_iota(jnp.int32, (tq, tq), 0)
    col = lax.broadcasted_iota(jnp.int32, (tq, tq), 1)

    def kv_block(n):
        start = pl.multiple_of(n * tq, tq)
        return kb_ref[pl.ds(start, tq), :], vb_ref[pl.ds(start, tq), :]

    k0, v0 = kv_block(qi)
    s = jnp.where(col <= row, _dot_nt(qh, k0) * SCALE, NEG)
    m = jnp.max(s, axis=-1, keepdims=True)
    p = jnp.exp(s - m)
    l = jnp.sum(p, axis=-1, keepdims=True)
    acc = _dot(p.astype(BF16), v0)

    def body(n, st):
        m, l, acc = st
        picked = jnp.sum(jnp.where(bcol == n, sel, 0.0), axis=-1, keepdims=True) > 0.5
        k, v = kv_block(n)
        s = jnp.where(picked, _dot_nt(qh, k) * SCALE, NEG)
        m_new = jnp.maximum(m, jnp.max(s, axis=-1, keepdims=True))
        alpha = jnp.exp(m - m_new)
        p = jnp.exp(s - m_new)
        l = alpha * l + jnp.sum(p, axis=-1, keepdims=True)
        acc = alpha * acc + _dot(p.astype(BF16), v)
        return m_new, l, acc

    m, l, acc = lax.fori_loop(0, qi, body, (m, l, acc))
    o_ref[...] = _head_rms(acc / l, g_ref[...]).astype(BF16)


def _moba_prompt(q, kv, g, n_batch, t):
    rows = q.shape[0]
    tq = MOBA_BLOCK
    nb = t // tq
    assert nb <= LANES
    return pl.pallas_call(
        functools.partial(_moba_prompt_kernel, nb=nb),
        grid=(n_batch, N_HEADS_MB, nb),
        in_specs=[
            pl.BlockSpec((tq, HEAD_DIM), lambda b, h, i: (b * nb + i, h)),
            pl.BlockSpec((t, HEAD_DIM), lambda b, h, i: (b, h)),
            pl.BlockSpec((t, HEAD_DIM), lambda b, h, i: (b, N_HEADS_MB + h)),
            pl.BlockSpec((1, HEAD_DIM), lambda b, h, i: (0, h)),
        ],
        out_specs=pl.BlockSpec((tq, HEAD_DIM), lambda b, h, i: (b * nb + i, h)),
        out_shape=jax.ShapeDtypeStruct((rows, D_MB), BF16),
        scratch_shapes=[pltpu.VMEM((t, HEAD_DIM), BF16), pltpu.VMEM((t, HEAD_DIM), BF16),
                        pltpu.VMEM((LANES, HEAD_DIM), F32)],
        compiler_params=_cparams(("arbitrary", "arbitrary", "arbitrary")),
        name="moba_prompt",
    )(q, kv, kv, g)


def _head_match(m_rows, n_cols, nh):
    r = lax.broadcasted_iota(jnp.int32, (m_rows, n_cols), 0)
    c = lax.broadcasted_iota(jnp.int32, (m_rows, n_cols), 1)
    return (r % nh) == (c % nh), r // nh, c // nh


def _sb_rows(q, k_rows, v_rows, c, acc, tri, valid):
    m_rows, n = q.shape[0], k_rows.shape[0]
    z = _dot_nt(q, k_rows) * SCALE
    sp = _softplus(z)
    l1mb = jnp.where(valid, -sp, 0.0)
    tiles = [l1mb[:, j * LANES:(j + 1) * LANES] for j in range(n // LANES)]
    later = [None] * len(tiles)
    for j in reversed(range(len(tiles))):
        later[j] = c
        c = c + jnp.sum(tiles[j], axis=-1, keepdims=True)
    hi, lo = _split_bf16(jnp.concatenate(tiles, axis=0))
    within = _dot(hi, tri) + _dot(lo, tri)
    after = jnp.concatenate(
        [within[j * m_rows:(j + 1) * m_rows] + later[j] for j in range(len(tiles))], axis=1)
    w = jnp.where(valid, jnp.exp(z - sp + after), 0.0)
    return c, acc + _dot(w.astype(BF16), v_rows)


def _sb_sample_kernel(pt_ref, q_ref, kn_ref, vn_ref, g_ref, cache_ref, o_ref,
                      buf_ref, newk_ref, newv_ref, sem, *, n_pages):
    b = pl.program_id(0)
    nh = N_HEADS_SB
    m_rows = q_ref.shape[1]
    n_new = kn_ref.shape[1]
    page_rows = PAGE_SIZE * nh

    def page_copy(p, slot):
        return pltpu.make_async_copy(cache_ref.at[pt_ref[b, p]], buf_ref.at[slot], sem.at[slot])

    page_copy(n_pages - 1, 0).start()

    newk_ref[...] = jnp.zeros(newk_ref.shape, F32)
    newv_ref[...] = jnp.zeros(newv_ref.shape, F32)
    newk_ref[0:n_new, :] = kn_ref[0]
    newv_ref[0:n_new, :] = vn_ref[0]

    q = q_ref[0]
    tri = _tri(LANES)

    match, qt, kt = _head_match(m_rows, LANES, nh)
    c, acc = _sb_rows(q, newk_ref[...].astype(BF16), newv_ref[...].astype(BF16),
                      jnp.zeros((m_rows, 1), F32), jnp.zeros((m_rows, HEAD_DIM), F32),
                      tri, match & (kt < qt))
    page_match, _, _ = _head_match(m_rows, page_rows, nh)

    def cond(st):
        p, c, _ = st
        return (p >= 0) & (jnp.max(c) > SB_SKIP_BELOW)

    def body(st):
        p, c, acc = st
        slot = (n_pages - 1 - p) % 2
        page_copy(p, slot).wait()

        @pl.when(p >= 1)
        def _():
            page_copy(p - 1, 1 - slot).start()

        k_rows = buf_ref[slot, :, 0].reshape(page_rows, HEAD_DIM).astype(BF16)
        v_rows = buf_ref[slot, :, 1].reshape(page_rows, HEAD_DIM).astype(BF16)
        c, acc = _sb_rows(q, k_rows, v_rows, c, acc, tri, page_match)
        return p - 1, c, acc

    p_end, _, acc = lax.while_loop(cond, body, (n_pages - 1, c, acc))

    @pl.when(p_end >= 0)
    def _():
        page_copy(p_end, (n_pages - 1 - p_end) % 2).wait()

    o_ref[0] = _head_rms(acc, g_ref[...]).astype(BF16)


def _sb_sample(page_table, q, k_new, v_new, g_rows, cache):
    nb, n_pages = page_table.shape
    m_rows, n_new = q.shape[1], k_new.shape[1]
    nh = cache.shape[3]
    assert n_new <= LANES and m_rows % 8 == 0
    grid_spec = pltpu.PrefetchScalarGridSpec(
        num_scalar_prefetch=1,
        grid=(nb,),
        in_specs=[
            pl.BlockSpec((1, m_rows, HEAD_DIM), lambda b, pt: (b, 0, 0)),
            pl.BlockSpec((1, n_new, HEAD_DIM), lambda b, pt: (b, 0, 0)),
            pl.BlockSpec((1, n_new, HEAD_DIM), lambda b, pt: (b, 0, 0)),
            pl.BlockSpec((m_rows, HEAD_DIM), lambda b, pt: (0, 0)),
            pl.BlockSpec(memory_space=pl.ANY),
        ],
        out_specs=pl.BlockSpec((1, m_rows, HEAD_DIM), lambda b, pt: (b, 0, 0)),
        scratch_shapes=[pltpu.VMEM((2, PAGE_SIZE, 2, nh, HEAD_DIM), F32),
                        pltpu.VMEM((LANES, HEAD_DIM), F32),
                        pltpu.VMEM((LANES, HEAD_DIM), F32),
                        pltpu.SemaphoreType.DMA((2,))],
    )
    return pl.pallas_call(
        functools.partial(_sb_sample_kernel, n_pages=n_pages),
        grid_spec=grid_spec,
        out_shape=jax.ShapeDtypeStruct((nb, m_rows, HEAD_DIM), BF16),
        compiler_params=_cparams(("arbitrary",)),
        name="sb_sample",
    )(page_table, q, k_new, v_new, g_rows, cache)


def _moba_sample_kernel(pt_ref, q_ref, kn_ref, vn_ref, g_ref, page_ref, o_ref,
                        newk_ref, newv_ref, m_ref, l_ref, s_ref, oall_ref, *, n_pages):
    p = pl.program_id(1)
    nh = N_HEADS_MB
    m_rows = q_ref.shape[1]
    n_new = kn_ref.shape[1]
    page_rows = PAGE_SIZE * nh
    half = LANES // 2

    qh, ql = _split_bf16(q_ref[0])
    qq = jnp.concatenate([qh, ql], axis=0)

    def partial_softmax(k_rows, v_rows, valid):
        qk2 = _dot_nt(qq, k_rows)
        qk = qk2[0:m_rows] + qk2[m_rows:2 * m_rows]
        s = jnp.where(valid, qk * SCALE, NEG)
        m = jnp.max(s, axis=-1, keepdims=True)
        e = jnp.exp(s - m)
        l = jnp.sum(e, axis=-1, keepdims=True)
        o = _dot(e.astype(BF16), v_rows)
        return jnp.sum(jnp.where(valid, qk, 0.0), axis=-1, keepdims=True), m, l, o

    lane = lax.broadcasted_iota(jnp.int32, (m_rows, LANES), 1)

    @pl.when(p == 0)
    def _():
        m_ref[...] = jnp.full(m_ref.shape, NEG, F32)
        l_ref[...] = jnp.zeros(l_ref.shape, F32)
        s_ref[...] = jnp.zeros(s_ref.shape, F32)

    page_match, _, _ = _head_match(m_rows, page_rows, nh)
    k_rows = page_ref[0, :, 0].reshape(page_rows, HEAD_DIM).astype(BF16)
    v_rows = page_ref[0, :, 1].reshape(page_rows, HEAD_DIM).astype(BF16)
    qsum, m, l, o = partial_softmax(k_rows, v_rows, page_match)
    here = lane == (p // 2 + (p % 2) * half)
    m_ref[...] = jnp.where(here, m, m_ref[...])
    l_ref[...] = jnp.where(here, l, l_ref[...])
    s_ref[...] = jnp.where(here, qsum, s_ref[...])
    oall_ref[p] = o

    @pl.when(p == n_pages - 1)
    def _():
        n_blocks = n_pages * PAGE_SIZE // MOBA_BLOCK
        s_all = s_ref[...]
        gate = (s_all + pltpu.roll(s_all, half, axis=1)) * (1.0 / MOBA_BLOCK)
        sel = _topk_mask(gate, n_blocks)
        sel = sel + pltpu.roll(sel, half, axis=1)

        newk_ref[...] = jnp.zeros(newk_ref.shape, F32)
        newv_ref[...] = jnp.zeros(newv_ref.shape, F32)
        newk_ref[0:n_new, :] = kn_ref[0]
        newv_ref[0:n_new, :] = vn_ref[0]
        match, qt, kt = _head_match(m_rows, LANES, nh)
        _, m_own, l_own, o_own = partial_softmax(newk_ref[...].astype(BF16), newv_ref[...].astype(BF16),
                                                 match & (kt <= qt))

        m_all = m_ref[...]
        m_tot = jnp.maximum(jnp.max(jnp.where(sel > 0.5, m_all, NEG), axis=-1, keepdims=True), m_own)
        wgt = jnp.where(sel > 0.5, jnp.exp(m_all - m_tot), 0.0)
        w_own = jnp.exp(m_own - m_tot)
        l_tot = jnp.sum(wgt * l_ref[...], axis=-1, keepdims=True) + w_own * l_own

        def merge(pp, acc):
            w = jnp.sum(jnp.where(lane == (pp // 2 + (pp % 2) * half), wgt, 0.0), axis=-1, keepdims=True)
            return acc + w * oall_ref[pp]

        acc = lax.fori_loop(0, n_pages, merge, w_own * o_own)
        o_ref[0] = _head_rms(acc / l_tot, g_ref[...]).astype(BF16)


def _moba_sample(page_table, q, k_new, v_new, g_rows, cache):
    nb, n_pages = page_table.shape
    m_rows, n_new = q.shape[1], k_new.shape[1]
    nh = cache.shape[3]
    assert n_pages % 2 == 0 and n_pages <= LANES and MOBA_BLOCK == 2 * PAGE_SIZE and n_new <= LANES
    grid_spec = pltpu.PrefetchScalarGridSpec(
        num_scalar_prefetch=1,
        grid=(nb, n_pages),
        in_specs=[
            pl.BlockSpec((1, m_rows, HEAD_DIM), lambda b, p, pt: (b, 0, 0)),
            pl.BlockSpec((1, n_new, HEAD_DIM), lambda b, p, pt: (b, 0, 0)),
            pl.BlockSpec((1, n_new, HEAD_DIM), lambda b, p, pt: (b, 0, 0)),
            pl.BlockSpec((m_rows, HEAD_DIM), lambda b, p, pt: (0, 0)),
            pl.BlockSpec((1, PAGE_SIZE, 2, nh, HEAD_DIM), lambda b, p, pt: (pt[b, p], 0, 0, 0, 0)),
        ],
        out_specs=pl.BlockSpec((1, m_rows, HEAD_DIM), lambda b, p, pt: (b, 0, 0)),
        scratch_shapes=[pltpu.VMEM((LANES, HEAD_DIM), F32),
                        pltpu.VMEM((LANES, HEAD_DIM), F32),
                        pltpu.VMEM((m_rows, LANES), F32),
                        pltpu.VMEM((m_rows, LANES), F32),
                        pltpu.VMEM((m_rows, LANES), F32),
                        pltpu.VMEM((n_pages, m_rows, HEAD_DIM), F32)],
    )
    return pl.pallas_call(
        functools.partial(_moba_sample_kernel, n_pages=n_pages),
        grid_spec=grid_spec,
        out_shape=jax.ShapeDtypeStruct((nb, m_rows, HEAD_DIM), BF16),
        compiler_params=_cparams(("arbitrary", "arbitrary")),
        name="moba_sample",
    )(page_table, q, k_new, v_new, g_rows, cache)


def _out_proj_kernel(osb_ref, omb_ref, h_ref, wa_ref, wb_ref, g_ref, b_ref, h1_ref, h1b_ref, *, alpha):
    mix = _dot(osb_ref[...], wa_ref[...]) + _dot(omb_ref[...], wb_ref[...])
    h1 = _layer_norm(alpha * h_ref[...] + mix, g_ref[...], b_ref[...])
    h1_ref[...] = h1
    h1b_ref[...] = h1.astype(BF16)


def _out_proj(o_sb, o_mb, h, w_out_bf16, g, b, tm, alpha):
    rows, d = h.shape
    return pl.pallas_call(
        functools.partial(_out_proj_kernel, alpha=alpha),
        grid=(rows // tm,),
        in_specs=[
            pl.BlockSpec((tm, D_SB), lambda i: (i, 0)),
            pl.BlockSpec((tm, D_MB), lambda i: (i, 0)),
            pl.BlockSpec((tm, d), lambda i: (i, 0)),
            pl.BlockSpec((D_SB, d), lambda i: (0, 0)),
            pl.BlockSpec((D_MB, d), lambda i: (1, 0)),
            pl.BlockSpec((1, d), lambda i: (0, 0)),
            pl.BlockSpec((1, d), lambda i: (0, 0)),
        ],
        out_specs=[pl.BlockSpec((tm, d), lambda i: (i, 0)), pl.BlockSpec((tm, d), lambda i: (i, 0))],
        out_shape=[jax.ShapeDtypeStruct((rows, d), F32), jax.ShapeDtypeStruct((rows, d), BF16)],
        compiler_params=_cparams(("arbitrary",)),
        name="out_proj",
    )(o_sb, o_mb, h, w_out_bf16, w_out_bf16, g, b)


HALO = 16


def _conv_ffn_kernel(*refs, alpha, seq_len, tail, has_state):
    if has_state:
        (h1_ref, h1b_ref, halo_ref, wg_ref, wu_ref, wd_ref, cw_ref, cb_ref, g_ref, b_ref, p1_ref, p2_ref,
         out_ref, gt_ref, x_ref, gbuf_ref, acc_ref) = refs
    else:
        (h1_ref, h1b_ref, halo_ref, wg_ref, wu_ref, wd_ref, cw_ref, cb_ref, g_ref, b_ref,
         out_ref, gt_ref, x_ref, gbuf_ref, acc_ref) = refs
    i = pl.program_id(0)
    j = pl.program_id(1)
    tm = h1_ref.shape[0]

    @pl.when(j == 0)
    def _():
        x_ref[0:HALO, :] = halo_ref[...]
        x_ref[HALO:HALO + tm, :] = h1b_ref[...]
        acc_ref[...] = jnp.zeros(acc_ref.shape, F32)

    gbuf_ref[...] = _dot(x_ref[...], wg_ref[...])
    u = _dot(x_ref[HALO:HALO + tm, :], wu_ref[...])
    g0 = gbuf_ref[HALO:HALO + tm, :]
    g1 = gbuf_ref[HALO - 1:HALO - 1 + tm, :]
    g2 = gbuf_ref[HALO - 2:HALO - 2 + tm, :]
    pos = (i * tm + lax.broadcasted_iota(jnp.int32, (tm, 1), 0)) % seq_len
    g1 = jnp.where(pos >= 1, g1, p1_ref[...] if has_state else 0.0)
    g2 = jnp.where(pos >= 2, g2, p2_ref[...] if has_state else 0.0)
    cw = cw_ref[...]
    c = cb_ref[...] + cw[0:1, :] * g2 + cw[1:2, :] * g1 + cw[2:3, :] * g0
    a = 0.5 * c * (1.0 + lax.erf(c * math.sqrt(0.5))) * u
    acc_ref[...] += _dot(a.astype(BF16), wd_ref[...])
    gt_ref[0] = g0[tm - tail:, :]

    @pl.when(j == pl.num_programs(1) - 1)
    def _():
        out_ref[...] = _layer_norm(alpha * h1_ref[...] + acc_ref[...], g_ref[...], b_ref[...])


def _conv_ffn(h1, h1b, wg, wu, wd, conv_w, conv_b, g, b, tm, tf, seq_len, tail, alpha, state=None):
    rows, d = h1.shape
    f = wg.shape[1]
    has_state = state is not None
    in_specs = [
        pl.BlockSpec((tm, d), lambda i, j: (i, 0)),
        pl.BlockSpec((tm, d), lambda i, j: (i, 0)),
        pl.BlockSpec((HALO, d), lambda i, j: (jnp.maximum(i * (tm // HALO) - 1, 0), 0)),
        pl.BlockSpec((d, tf), lambda i, j: (0, j)),
        pl.BlockSpec((d, tf), lambda i, j: (0, j)),
        pl.BlockSpec((tf, d), lambda i, j: (j, 0)),
        pl.BlockSpec((CONV_WIDTH, tf), lambda i, j: (0, j)),
        pl.BlockSpec((1, tf), lambda i, j: (0, j)),
        pl.BlockSpec((1, d), lambda i, j: (0, 0)),
        pl.BlockSpec((1, d), lambda i, j: (0, 0)),
    ]
    args = [h1, h1b, h1b, wg, wu, wd, conv_w, conv_b, g, b]
    if has_state:
        in_specs += [pl.BlockSpec((tm, tf), lambda i, j: (i, j)), pl.BlockSpec((tm, tf), lambda i, j: (i, j))]
        args += list(state)
    return pl.pallas_call(
        functools.partial(_conv_ffn_kernel, alpha=alpha, seq_len=seq_len, tail=tail, has_state=has_state),
        grid=(rows // tm, f // tf),
        in_specs=in_specs,
        out_specs=[pl.BlockSpec((tm, d), lambda i, j: (i, 0)),
                   pl.BlockSpec((1, tail, tf), lambda i, j: (i, 0, j))],
        out_shape=[jax.ShapeDtypeStruct((rows, d), F32),
                   jax.ShapeDtypeStruct((rows // tm, tail, f), F32)],
        scratch_shapes=[pltpu.VMEM((HALO + tm, d), BF16), pltpu.VMEM((HALO + tm, tf), F32),
                        pltpu.VMEM((tm, d), F32)],
        compiler_params=_cparams(("arbitrary", "arbitrary")),
        name="conv_ffn",
    )(*args)


def _rope_tables(pos):
    half = HEAD_DIM // 2
    inv = ROPE_THETA ** (-jnp.arange(half, dtype=F32) / half)
    ang = pos.astype(F32)[:, None] * inv[None, :]
    cos, sin = jnp.cos(ang), jnp.sin(ang)
    return jnp.concatenate([cos, cos], axis=-1), jnp.concatenate([-sin, sin], axis=-1)


def _row(v):
    return v.reshape(1, -1)


def kernel(x_prompt, x_sample, cache_kv_sb, cache_kv_mb, state_conv, page_table, ln_in_g, ln_in_b, w_in, g_sb, g_mb, w_out, ln1_g, ln1_b, w_gate, w_up, conv_w, conv_b, w_down, ln2_g, ln2_b):
    depth = w_in.shape[0]
    n_p, t_p, d = x_prompt.shape
    n_s, t_s, _ = x_sample.shape
    f = w_gate.shape[-1]
    past = page_table.shape[1] * PAGE_SIZE
    alpha = (2 * depth) ** 0.25
    assert depth == 1, "the input LayerNorm is fused into the first layer's projection"
    assert t_p % MOBA_BLOCK == 0 and past % MOBA_BLOCK == 0
    l = 0

    tm_p, tf = 512, 512
    rows_p, rows_s = n_p * t_p, n_s * t_s
    cos_p, sin_p = _rope_tables(jnp.arange(t_p, dtype=jnp.int32))
    cos_s, sin_s = _rope_tables(past + jnp.arange(t_s, dtype=jnp.int32))
    cos_s, sin_s = jnp.tile(cos_s, (n_s, 1)), jnp.tile(sin_s, (n_s, 1))

    ln_g, ln_b = _row(ln_in_g), _row(ln_in_b)
    w_in_b = w_in[l].astype(BF16)
    w_out_b = w_out[l].astype(BF16)
    wg_b, wu_b, wd_b = w_gate[l].astype(BF16), w_up[l].astype(BF16), w_down[l].astype(BF16)
    ffn_consts = (conv_w[l], _row(conv_b[l]), _row(ln2_g[l]), _row(ln2_b[l]))

    h, qsb, kvsb, qmb, kvmb = _ln_qkv(x_prompt.reshape(rows_p, d), ln_g, ln_b, cos_p, sin_p, w_in_b,
                                      tm_p, t_p // tm_p)
    o_sb = _sb_prompt(qsb, kvsb, _row(g_sb[l]), n_p, t_p)
    o_mb = _moba_prompt(qmb, kvmb, _row(g_mb[l]), n_p, t_p)
    h1, h1b = _out_proj(o_sb, o_mb, h, w_out_b, _row(ln1_g[l]), _row(ln1_b[l]), tm_p, alpha)
    y_p, g_tail = _conv_ffn(h1, h1b, wg_b, wu_b, wd_b, *ffn_consts, tm_p, tf, t_p, 8, alpha)
    kv_sb_p = kvsb.reshape(n_p, t_p, 2, N_HEADS_SB, HEAD_DIM)
    kv_mb_p = kvmb.reshape(n_p, t_p, 2, N_HEADS_MB, HEAD_DIM)
    tiles_per_seq = t_p // tm_p
    conv_p = g_tail[tiles_per_seq - 1::tiles_per_seq, 8 - (CONV_WIDTH - 1):, :]

    h, qsb, kvsb, qmb, kvmb = _ln_qkv(x_sample.reshape(rows_s, d), ln_g, ln_b, cos_s, sin_s, w_in_b, rows_s, 1)
    kv_sb_s = kvsb.reshape(n_s, t_s, 2, N_HEADS_SB, HEAD_DIM)
    kv_mb_s = kvmb.reshape(n_s, t_s, 2, N_HEADS_MB, HEAD_DIM)
    o_sb = _sb_sample(page_table, qsb.reshape(n_s, t_s * N_HEADS_SB, HEAD_DIM),
                      kv_sb_s[:, :, 0].reshape(n_s, t_s * N_HEADS_SB, HEAD_DIM),
                      kv_sb_s[:, :, 1].reshape(n_s, t_s * N_HEADS_SB, HEAD_DIM),
                      jnp.tile(g_sb[l].reshape(N_HEADS_SB, HEAD_DIM), (t_s, 1)), cache_kv_sb[l])
    o_mb = _moba_sample(page_table, qmb.reshape(n_s, t_s * N_HEADS_MB, HEAD_DIM),
                        kv_mb_s[:, :, 0].reshape(n_s, t_s * N_HEADS_MB, HEAD_DIM),
                        kv_mb_s[:, :, 1].reshape(n_s, t_s * N_HEADS_MB, HEAD_DIM),
                        jnp.tile(g_mb[l].reshape(N_HEADS_MB, HEAD_DIM), (t_s, 1)), cache_kv_mb[l])
    h1, h1b = _out_proj(o_sb.reshape(rows_s, D_SB), o_mb.reshape(rows_s, D_MB), h, w_out_b,
                        _row(ln1_g[l]), _row(ln1_b[l]), rows_s, alpha)
    sc = state_conv[l].astype(F32)
    zero = jnp.zeros((n_s, t_s, f), F32)
    prev1 = zero.at[:, 0].set(sc[:, 1]).reshape(rows_s, f)
    prev2 = zero.at[:, 0].set(sc[:, 0]).at[:, 1].set(sc[:, 1]).reshape(rows_s, f)
    y_s, g_all = _conv_ffn(h1, h1b, wg_b, wu_b, wd_b, *ffn_consts, rows_s, tf, t_s, rows_s, alpha,
                           state=(prev1, prev2))
    conv_s = g_all.reshape(n_s, t_s, f)[:, t_s - (CONV_WIDTH - 1):]

    return (y_p.reshape(n_p, t_p, d), y_s.reshape(n_s, t_s, d), kv_sb_p[None], kv_mb_p[None], conv_p[None],
            kv_sb_s[None], kv_mb_s[None], conv_s[None])
```

```python
import functools
import math

import jax
import jax.numpy as jnp
from jax import lax
from jax.experimental import pallas as pl
from jax.experimental.pallas import tpu as pltpu

F32 = jnp.float32
BF16 = jnp.bfloat16

HEAD_DIM = 128
N_HEADS_SB = 8
N_HEADS_MB = 8
D_SB = N_HEADS_SB * HEAD_DIM
D_MB = N_HEADS_MB * HEAD_DIM
MOBA_BLOCK = 256
MOBA_TOPK = 3
ROPE_THETA = 10000.0
CONV_WIDTH = 3
PAGE_SIZE = 128
LN_EPS = 1e-5
RMS_EPS = 1e-6
NEG = -1e30
SCALE = 1.0 / math.sqrt(HEAD_DIM)

SB_SKIP_BELOW = -105.0

SB_TQ = 128
LANES = 128
VMEM_LIMIT_V7X = 56 * 1024 * 1024


def _cparams(sem):
    return pltpu.CompilerParams(dimension_semantics=sem, vmem_limit_bytes=VMEM_LIMIT_V7X)


def _layer_norm(x, g, b):
    mu = jnp.mean(x, axis=-1, keepdims=True)
    xc = x - mu
    var = jnp.mean(xc * xc, axis=-1, keepdims=True)
    return xc * lax.rsqrt(var + LN_EPS) * g + b


def _head_rms(o, g):
    ms = jnp.mean(o * o, axis=-1, keepdims=True)
    return o * lax.rsqrt(ms + RMS_EPS) * g


def _dot_nt(a, b):
    return lax.dot_general(a, b, (((1,), (1,)), ((), ())), preferred_element_type=F32)


def _dot(a, b):
    return jnp.dot(a, b, preferred_element_type=F32)


def _split_bf16(x):
    hi = x.astype(BF16)
    lo = (x - hi.astype(F32)).astype(BF16)
    return hi, lo


def _softplus(z):
    return jnp.maximum(z, 0.0) + jnp.log1p(jnp.exp(-jnp.abs(z)))


def _tri(n):
    r = lax.broadcasted_iota(jnp.int32, (n, n), 0)
    s = lax.broadcasted_iota(jnp.int32, (n, n), 1)
    return (r > s).astype(BF16)


QKV_TN = 512
QKV_STEPS_PER_SEG = D_SB // QKV_TN

def _ln_qkv_kernel(x_ref, g_ref, b_ref, cos_ref, sin_ref, w_ref, h_ref, qsb_ref, kvsb_ref, qmb_ref, kvmb_ref,
                   *rest, attention_copies):
    if attention_copies:
        kvsb_bf_ref, kvmb_bf_ref, kmean_ref, hb_ref = rest
    else:
        (hb_ref,) = rest
    j = pl.program_id(1)

    @pl.when(j == 0)
    def _():
        h = _layer_norm(x_ref[...], g_ref[...], b_ref[...])
        h_ref[...] = h
        hb_ref[...] = h.astype(BF16)

    seg = j // QKV_STEPS_PER_SEG
    acc = _dot(hb_ref[...], w_ref[...])

    def rope(a):
        cos = cos_ref[...]
        sin = sin_ref[...]
        heads = []
        for hh in range(a.shape[1] // HEAD_DIM):
            ah = a[:, hh * HEAD_DIM:(hh + 1) * HEAD_DIM]
            heads.append(ah * cos + pltpu.roll(ah, HEAD_DIM // 2, axis=1) * sin)
        return jnp.concatenate(heads, axis=1)

    @pl.when(seg == 0)
    def _():
        qsb_ref[...] = acc.astype(BF16)

    @pl.when((seg == 1) | (seg == 2))
    def _():
        kvsb_ref[...] = acc
        if attention_copies:
            kvsb_bf_ref[...] = acc.astype(BF16)

    @pl.when(seg == 3)
    def _():
        qmb_ref[...] = rope(acc)

    @pl.when(seg == 4)
    def _():
        k = rope(acc)
        kvmb_ref[...] = k
        if attention_copies:
            kvmb_bf_ref[...] = k.astype(BF16)
            for r in range(k.shape[0] // MOBA_BLOCK):
                kmean_ref[0, r:r + 1, :] = jnp.mean(k[r * MOBA_BLOCK:(r + 1) * MOBA_BLOCK, :], axis=0,
                                                    keepdims=True)

    @pl.when(seg == 5)
    def _():
        kvmb_ref[...] = acc
        if attention_copies:
            kvmb_bf_ref[...] = acc.astype(BF16)


def _ln_qkv(x, g, b, cos, sin, w_bf16, tm, n_pos_tiles, attention_copies):
    rows, d = x.shape
    assert w_bf16.shape[1] == 3 * D_SB + 3 * D_MB and D_SB == D_MB
    sps = QKV_STEPS_PER_SEG
    grid = (rows // tm, 6 * sps)
    col = lambda j, first_seg, n_seg: jnp.minimum(jnp.maximum(j - first_seg * sps, 0), n_seg * sps - 1)
    out_specs = [
        pl.BlockSpec((tm, d), lambda i, j: (i, 0)),
        pl.BlockSpec((tm, QKV_TN), lambda i, j: (i, col(j, 0, 1))),
        pl.BlockSpec((tm, QKV_TN), lambda i, j: (i, col(j, 1, 2))),
        pl.BlockSpec((tm, QKV_TN), lambda i, j: (i, col(j, 3, 1))),
        pl.BlockSpec((tm, QKV_TN), lambda i, j: (i, col(j, 4, 2))),
    ]
    out_shape = [
        jax.ShapeDtypeStruct((rows, d), F32),
        jax.ShapeDtypeStruct((rows, D_SB), BF16),
        jax.ShapeDtypeStruct((rows, 2 * D_SB), F32),
        jax.ShapeDtypeStruct((rows, D_MB), F32),
        jax.ShapeDtypeStruct((rows, 2 * D_MB), F32),
    ]
    if attention_copies:
        assert tm % MOBA_BLOCK == 0
        bpt = tm // MOBA_BLOCK
        out_specs += [
            pl.BlockSpec((tm, QKV_TN), lambda i, j: (i, col(j, 1, 2))),
            pl.BlockSpec((tm, QKV_TN), lambda i, j: (i, col(j, 4, 2))),
            pl.BlockSpec((1, bpt, QKV_TN), lambda i, j: (i, 0, col(j, 4, 1))),
        ]
        out_shape += [
            jax.ShapeDtypeStruct((rows, 2 * D_SB), BF16),
            jax.ShapeDtypeStruct((rows, 2 * D_MB), BF16),
            jax.ShapeDtypeStruct((rows // tm, bpt, D_MB), F32),
        ]
    return pl.pallas_call(
        functools.partial(_ln_qkv_kernel, attention_copies=attention_copies),
        grid=grid,
        in_specs=[
            pl.BlockSpec((tm, d), lambda i, j: (i, 0)),
            pl.BlockSpec((1, d), lambda i, j: (0, 0)),
            pl.BlockSpec((1, d), lambda i, j: (0, 0)),
            pl.BlockSpec((tm, HEAD_DIM), lambda i, j: (i % n_pos_tiles, 0)),
            pl.BlockSpec((tm, HEAD_DIM), lambda i, j: (i % n_pos_tiles, 0)),
            pl.BlockSpec((d, QKV_TN), lambda i, j: (0, j)),
        ],
        out_specs=out_specs,
        out_shape=out_shape,
        scratch_shapes=[pltpu.VMEM((tm, d), BF16)],
        compiler_params=_cparams(("arbitrary", "arbitrary")),
        name="ln_qkv",
    )(x, g, b, cos, sin, w_bf16)


def _sb_prompt_kernel(q_ref, kv_ref, g_ref, o_ref, c_ref, acc_ref):
    qi = pl.program_id(1)
    nh, tq = N_HEADS_SB, SB_TQ
    tri = _tri(tq)
    row = lax.broadcasted_iota(jnp.int32, (nh * tq, tq), 0) % tq
    col = lax.broadcasted_iota(jnp.int32, (nh * tq, tq), 1)

    def head(h):
        return slice(h * HEAD_DIM, (h + 1) * HEAD_DIM)

    def attend(kb, valid):
        keys = pl.ds(pl.multiple_of(kb * tq, tq), tq)
        z = jnp.concatenate([_dot_nt(q_ref[:, head(h)], kv_ref[keys, head(h)]) for h in range(nh)],
                            axis=0) * SCALE
        sp = _softplus(z)
        l1mb = -sp if valid is None else jnp.where(valid, -sp, 0.0)
        hi, lo = _split_bf16(l1mb)
        c = c_ref[...]
        after = _dot(hi, tri) + _dot(lo, tri) + c
        w = jnp.exp(z - sp + after)
        if valid is not None:
            w = jnp.where(valid, w, 0.0)
        wb = w.astype(BF16)
        acc_ref[...] += jnp.concatenate(
            [_dot(wb[h * tq:(h + 1) * tq, :], kv_ref[keys, head(nh + h)]) for h in range(nh)], axis=0)
        c = c + jnp.sum(l1mb, axis=-1, keepdims=True)
        c_ref[...] = c
        return jnp.max(c)

    c_ref[...] = jnp.zeros(c_ref.shape, F32)
    acc_ref[...] = jnp.zeros(acc_ref.shape, F32)
    c_max = attend(qi, col < row)

    def cond(st):
        kb, c_max = st
        return (kb >= 0) & (c_max > SB_SKIP_BELOW)

    def body(st):
        kb, _ = st
        return kb - 1, attend(kb, None)

    lax.while_loop(cond, body, (qi - 1, c_max))
    for h in range(nh):
        o_ref[:, head(h)] = _head_rms(acc_ref[h * tq:(h + 1) * tq, :], g_ref[:, head(h)]).astype(BF16)


def _sb_prompt(q, kv_bf16, g, n_batch, t):
    rows = q.shape[0]
    nq = t // SB_TQ
    return pl.pallas_call(
        _sb_prompt_kernel,
        grid=(n_batch, nq),
        in_specs=[
            pl.BlockSpec((SB_TQ, D_SB), lambda b, i: (b * nq + i, 0)),
            pl.BlockSpec((t, 2 * D_SB), lambda b, i: (b, 0)),
            pl.BlockSpec((1, D_SB), lambda b, i: (0, 0)),
        ],
        out_specs=pl.BlockSpec((SB_TQ, D_SB), lambda b, i: (b * nq + i, 0)),
        out_shape=jax.ShapeDtypeStruct((rows, D_SB), BF16),
        scratch_shapes=[pltpu.VMEM((N_HEADS_SB * SB_TQ, 1), F32),
                        pltpu.VMEM((N_HEADS_SB * SB_TQ, HEAD_DIM), F32)],
        compiler_params=_cparams(("arbitrary", "arbitrary")),
        name="sb_prompt",
    )(q, kv_bf16, g)


def _topk_mask(gate, n_valid):
    col = lax.broadcasted_iota(jnp.int32, gate.shape, 1)
    colf = col.astype(F32)
    g = jnp.where(col < n_valid, gate, NEG)
    sel = jnp.zeros(gate.shape, F32)
    for _ in range(MOBA_TOPK):
        m = jnp.max(g, axis=-1, keepdims=True)
        first = jnp.min(jnp.where(g == m, colf, float(gate.shape[1])), axis=-1, keepdims=True)
        pick = (colf == first) & (m > 0.5 * NEG)
        sel = jnp.where(pick, 1.0, sel)
        g = jnp.where(pick, NEG, g)
    return sel


MOBA_CHUNK = 4


def _moba_prompt_kernel(q_ref, k_ref, v_ref, kmean_ref, g_ref, o_ref, km_ref, *, nb):
    qi = pl.program_id(2)
    tq = MOBA_BLOCK

    @pl.when(qi == 0)
    def _():
        km_ref[...] = jnp.zeros(km_ref.shape, F32)
        km_ref[0:nb, :] = kmean_ref[0]

    qh = q_ref[...].astype(BF16)
    gate = _dot_nt(qh, km_ref[...].astype(BF16))
    sel = _topk_mask(gate, qi)
    bcol = lax.broadcasted_iota(jnp.int32, (tq, LANES), 1)

    row = lax.broadcasted_iota(jnp.int32, (tq, tq), 0)
    col = lax.broadcasted_iota(jnp.int32, (tq, tq), 1)

    own = pl.ds(pl.multiple_of(qi * tq, tq), tq)
    s = jnp.where(col <= row, _dot_nt(qh, k_ref[own, :]) * SCALE, NEG)
    m = jnp.max(s, axis=-1, keepdims=True)
    p = jnp.exp(s - m)
    l = jnp.sum(p, axis=-1, keepdims=True)
    acc = _dot(p.astype(BF16), v_ref[own, :])

    def body(j, st):
        m, l, acc = st
        keys = pl.ds(pl.multiple_of(j * (MOBA_CHUNK * tq), MOBA_CHUNK * tq), MOBA_CHUNK * tq)
        s = _dot_nt(qh, k_ref[keys, :]) * SCALE
        parts = []
        for c in range(MOBA_CHUNK):
            picked = jnp.sum(jnp.where(bcol == j * MOBA_CHUNK + c, sel, 0.0), axis=-1, keepdims=True) > 0.5
            parts.append(jnp.where(picked, s[:, c * tq:(c + 1) * tq], NEG))
        s = jnp.concatenate(parts, axis=1)
        m_new = jnp.maximum(m, jnp.max(s, axis=-1, keepdims=True))
        alpha = jnp.exp(m - m_new)
        p = jnp.exp(s - m_new)
        l = alpha * l + jnp.sum(p, axis=-1, keepdims=True)
        acc = alpha * acc + _dot(p.astype(BF16), v_ref[keys, :])
        return m_new, l, acc

    m, l, acc = lax.fori_loop(0, (qi + MOBA_CHUNK - 1) // MOBA_CHUNK, body, (m, l, acc))
    o_ref[...] = _head_rms(acc / l, g_ref[...]).astype(BF16)


def _moba_prompt(q, kv_bf16, kmean, g, n_batch, t):
    rows = q.shape[0]
    tq = MOBA_BLOCK
    nb = t // tq
    assert nb <= LANES and nb % MOBA_CHUNK == 0
    return pl.pallas_call(
        functools.partial(_moba_prompt_kernel, nb=nb),
        grid=(n_batch, N_HEADS_MB, nb),
        in_specs=[
            pl.BlockSpec((tq, HEAD_DIM), lambda b, h, i: (b * nb + i, h)),
            pl.BlockSpec((t, HEAD_DIM), lambda b, h, i: (b, h)),
            pl.BlockSpec((t, HEAD_DIM), lambda b, h, i: (b, N_HEADS_MB + h)),
            pl.BlockSpec((1, nb, HEAD_DIM), lambda b, h, i: (b, 0, h)),
            pl.BlockSpec((1, HEAD_DIM), lambda b, h, i: (0, h)),
        ],
        out_specs=pl.BlockSpec((tq, HEAD_DIM), lambda b, h, i: (b * nb + i, h)),
        out_shape=jax.ShapeDtypeStruct((rows, D_MB), BF16),
        scratch_shapes=[pltpu.VMEM((LANES, HEAD_DIM), F32)],
        compiler_params=_cparams(("arbitrary", "arbitrary", "arbitrary")),
        name="moba_prompt",
    )(q, kv_bf16, kv_bf16, kmean, g)


def _head_match(m_rows, n_cols, nh):
    r = lax.broadcasted_iota(jnp.int32, (m_rows, n_cols), 0)
    c = lax.broadcasted_iota(jnp.int32, (m_rows, n_cols), 1)
    return (r % nh) == (c % nh), r // nh, c // nh


def _sb_rows(q, k_rows, v_rows, c, acc, tri, valid):
    m_rows, n = q.shape[0], k_rows.shape[0]
    z = _dot_nt(q, k_rows) * SCALE
    sp = _softplus(z)
    l1mb = jnp.where(valid, -sp, 0.0)
    tiles = [l1mb[:, j * LANES:(j + 1) * LANES] for j in range(n // LANES)]
    later = [None] * len(tiles)
    for j in reversed(range(len(tiles))):
        later[j] = c
        c = c + jnp.sum(tiles[j], axis=-1, keepdims=True)
    hi, lo = _split_bf16(jnp.concatenate(tiles, axis=0))
    within = _dot(hi, tri) + _dot(lo, tri)
    after = jnp.concatenate(
        [within[j * m_rows:(j + 1) * m_rows] + later[j] for j in range(len(tiles))], axis=1)
    w = jnp.where(valid, jnp.exp(z - sp + after), 0.0)
    return c, acc + _dot(w.astype(BF16), v_rows)


def _sb_sample_kernel(pt_ref, q_ref, kn_ref, vn_ref, g_ref, cache_ref, o_ref,
                      buf_ref, newk_ref, newv_ref, sem, *, n_pages):
    b = pl.program_id(0)
    nh = N_HEADS_SB
    m_rows = q_ref.shape[1]
    n_new = kn_ref.shape[1]
    page_rows = PAGE_SIZE * nh

    def page_copy(p, slot):
        return pltpu.make_async_copy(cache_ref.at[pt_ref[b, p]], buf_ref.at[slot], sem.at[slot])

    page_copy(n_pages - 1, 0).start()

    newk_ref[...] = jnp.zeros(newk_ref.shape, F32)
    newv_ref[...] = jnp.zeros(newv_ref.shape, F32)
    newk_ref[0:n_new, :] = kn_ref[0]
    newv_ref[0:n_new, :] = vn_ref[0]

    q = q_ref[0]
    tri = _tri(LANES)

    match, qt, kt = _head_match(m_rows, LANES, nh)
    c, acc = _sb_rows(q, newk_ref[...].astype(BF16), newv_ref[...].astype(BF16),
                      jnp.zeros((m_rows, 1), F32), jnp.zeros((m_rows, HEAD_DIM), F32),
                      tri, match & (kt < qt))
    page_match, _, _ = _head_match(m_rows, page_rows, nh)

    def cond(st):
        p, c, _ = st
        return (p >= 0) & (jnp.max(c) > SB_SKIP_BELOW)

    def body(st):
        p, c, acc = st
        slot = (n_pages - 1 - p) % 2
        page_copy(p, slot).wait()

        @pl.when(p >= 1)
        def _():
            page_copy(p - 1, 1 - slot).start()

        k_rows = buf_ref[slot, :, 0].reshape(page_rows, HEAD_DIM).astype(BF16)
        v_rows = buf_ref[slot, :, 1].reshape(page_rows, HEAD_DIM).astype(BF16)
        c, acc = _sb_rows(q, k_rows, v_rows, c, acc, tri, page_match)
        return p - 1, c, acc

    p_end, _, acc = lax.while_loop(cond, body, (n_pages - 1, c, acc))

    @pl.when(p_end >= 0)
    def _():
        page_copy(p_end, (n_pages - 1 - p_end) % 2).wait()

    o_ref[0] = _head_rms(acc, g_ref[...]).astype(BF16)


def _sb_sample(page_table, q, k_new, v_new, g_rows, cache):
    nb, n_pages = page_table.shape
    m_rows, n_new = q.shape[1], k_new.shape[1]
    nh = cache.shape[3]
    assert n_new <= LANES and m_rows % 8 == 0
    grid_spec = pltpu.PrefetchScalarGridSpec(
        num_scalar_prefetch=1,
        grid=(nb,),
        in_specs=[
            pl.BlockSpec((1, m_rows, HEAD_DIM), lambda b, pt: (b, 0, 0)),
            pl.BlockSpec((1, n_new, HEAD_DIM), lambda b, pt: (b, 0, 0)),
            pl.BlockSpec((1, n_new, HEAD_DIM), lambda b, pt: (b, 0, 0)),
            pl.BlockSpec((m_rows, HEAD_DIM), lambda b, pt: (0, 0)),
            pl.BlockSpec(memory_space=pl.ANY),
        ],
        out_specs=pl.BlockSpec((1, m_rows, HEAD_DIM), lambda b, pt: (b, 0, 0)),
        scratch_shapes=[pltpu.VMEM((2, PAGE_SIZE, 2, nh, HEAD_DIM), F32),
                        pltpu.VMEM((LANES, HEAD_DIM), F32),
                        pltpu.VMEM((LANES, HEAD_DIM), F32),
                        pltpu.SemaphoreType.DMA((2,))],
    )
    return pl.pallas_call(
        functools.partial(_sb_sample_kernel, n_pages=n_pages),
        grid_spec=grid_spec,
        out_shape=jax.ShapeDtypeStruct((nb, m_rows, HEAD_DIM), BF16),
        compiler_params=_cparams(("arbitrary",)),
        name="sb_sample",
    )(page_table, q, k_new, v_new, g_rows, cache)


MOBA_PAGES_PER_STEP = 8


def _moba_sample_kernel(pt_ref, q_ref, kn_ref, vn_ref, g_ref, *rest, n_pages):
    page_refs = rest[:MOBA_PAGES_PER_STEP]
    o_ref, newk_ref, newv_ref, m_ref, l_ref, s_ref, oall_ref = rest[MOBA_PAGES_PER_STEP:]
    step = pl.program_id(1)
    nh = N_HEADS_MB
    m_rows = q_ref.shape[1]
    n_new = kn_ref.shape[1]
    page_rows = PAGE_SIZE * nh
    ppb = MOBA_BLOCK // PAGE_SIZE
    bps = MOBA_PAGES_PER_STEP // ppb

    qh, ql = _split_bf16(q_ref[0])
    qq = jnp.concatenate([qh, ql], axis=0)

    def partial_softmax(k_rows, v_rows, valid):
        qk2 = _dot_nt(qq, k_rows)
        qk = qk2[0:m_rows] + qk2[m_rows:2 * m_rows]
        s = jnp.where(valid, qk * SCALE, NEG)
        m = jnp.max(s, axis=-1, keepdims=True)
        e = jnp.exp(s - m)
        l = jnp.sum(e, axis=-1, keepdims=True)
        o = _dot(e.astype(BF16), v_rows)
        return jnp.sum(jnp.where(valid, qk, 0.0), axis=-1, keepdims=True), m, l, o

    lane = lax.broadcasted_iota(jnp.int32, (m_rows, LANES), 1)

    @pl.when(step == 0)
    def _():
        m_ref[...] = jnp.full(m_ref.shape, NEG, F32)
        l_ref[...] = jnp.zeros(l_ref.shape, F32)
        s_ref[...] = jnp.zeros(s_ref.shape, F32)

    block_match, _, _ = _head_match(m_rows, ppb * page_rows, nh)
    m_all, l_all, s_all = m_ref[...], l_ref[...], s_ref[...]
    for i in range(bps):
        n = step * bps + i
        pages = page_refs[i * ppb:(i + 1) * ppb]
        k_rows = jnp.concatenate([r[0, :, 0].reshape(page_rows, HEAD_DIM).astype(BF16) for r in pages], axis=0)
        v_rows = jnp.concatenate([r[0, :, 1].reshape(page_rows, HEAD_DIM).astype(BF16) for r in pages], axis=0)
        qsum, m, l, o = partial_softmax(k_rows, v_rows, block_match)
        here = lane == n
        m_all = jnp.where(here, m, m_all)
        l_all = jnp.where(here, l, l_all)
        s_all = jnp.where(here, qsum, s_all)
        oall_ref[n] = o
    m_ref[...], l_ref[...], s_ref[...] = m_all, l_all, s_all

    @pl.when(step == pl.num_programs(1) - 1)
    def _():
        n_blocks = n_pages // ppb
        sel = _topk_mask(s_ref[...] * (1.0 / MOBA_BLOCK), n_blocks)

        newk_ref[...] = jnp.zeros(newk_ref.shape, F32)
        newv_ref[...] = jnp.zeros(newv_ref.shape, F32)
        newk_ref[0:n_new, :] = kn_ref[0]
        newv_ref[0:n_new, :] = vn_ref[0]
        match, qt, kt = _head_match(m_rows, LANES, nh)
        _, m_own, l_own, o_own = partial_softmax(newk_ref[...].astype(BF16), newv_ref[...].astype(BF16),
                                                 match & (kt <= qt))

        m_all = m_ref[...]
        m_tot = jnp.maximum(jnp.max(jnp.where(sel > 0.5, m_all, NEG), axis=-1, keepdims=True), m_own)
        wgt = jnp.where(sel > 0.5, jnp.exp(m_all - m_tot), 0.0)
        w_own = jnp.exp(m_own - m_tot)
        l_tot = jnp.sum(wgt * l_ref[...], axis=-1, keepdims=True) + w_own * l_own

        def merge(n, acc):
            w = jnp.sum(jnp.where(lane == n, wgt, 0.0), axis=-1, keepdims=True)
            return acc + w * oall_ref[n]

        acc = lax.fori_loop(0, n_blocks, merge, w_own * o_own)
        o_ref[0] = _head_rms(acc / l_tot, g_ref[...]).astype(BF16)


def _moba_sample(page_table, q, k_new, v_new, g_rows, cache):
    nb, n_pages = page_table.shape
    m_rows, n_new = q.shape[1], k_new.shape[1]
    nh = cache.shape[3]
    pps = MOBA_PAGES_PER_STEP
    assert n_pages <= LANES and MOBA_BLOCK == 2 * PAGE_SIZE and n_new <= LANES
    assert pps % 2 == 0 and n_pages % pps == 0
    page_spec = lambda i: pl.BlockSpec((1, PAGE_SIZE, 2, nh, HEAD_DIM),
                                       lambda b, s, pt: (pt[b, s * pps + i], 0, 0, 0, 0))
    grid_spec = pltpu.PrefetchScalarGridSpec(
        num_scalar_prefetch=1,
        grid=(nb, n_pages // pps),
        in_specs=[
            pl.BlockSpec((1, m_rows, HEAD_DIM), lambda b, s, pt: (b, 0, 0)),
            pl.BlockSpec((1, n_new, HEAD_DIM), lambda b, s, pt: (b, 0, 0)),
            pl.BlockSpec((1, n_new, HEAD_DIM), lambda b, s, pt: (b, 0, 0)),
            pl.BlockSpec((m_rows, HEAD_DIM), lambda b, s, pt: (0, 0)),
        ] + [page_spec(i) for i in range(pps)],
        out_specs=pl.BlockSpec((1, m_rows, HEAD_DIM), lambda b, s, pt: (b, 0, 0)),
        scratch_shapes=[pltpu.VMEM((LANES, HEAD_DIM), F32),
                        pltpu.VMEM((LANES, HEAD_DIM), F32),
                        pltpu.VMEM((m_rows, LANES), F32),
                        pltpu.VMEM((m_rows, LANES), F32),
                        pltpu.VMEM((m_rows, LANES), F32),
                        pltpu.VMEM((n_pages * PAGE_SIZE // MOBA_BLOCK, m_rows, HEAD_DIM), F32)],
    )
    return pl.pallas_call(
        functools.partial(_moba_sample_kernel, n_pages=n_pages),
        grid_spec=grid_spec,
        out_shape=jax.ShapeDtypeStruct((nb, m_rows, HEAD_DIM), BF16),
        compiler_params=_cparams(("arbitrary", "arbitrary")),
        name="moba_sample",
    )(page_table, q, k_new, v_new, g_rows, *([cache] * pps))


def _out_proj_kernel(osb_ref, omb_ref, h_ref, wa_ref, wb_ref, g_ref, b_ref, h1_ref, h1b_ref, *, alpha):
    mix = _dot(osb_ref[...], wa_ref[...]) + _dot(omb_ref[...], wb_ref[...])
    h1 = _layer_norm(alpha * h_ref[...] + mix, g_ref[...], b_ref[...])
    h1_ref[...] = h1
    h1b_ref[...] = h1.astype(BF16)


def _out_proj(o_sb, o_mb, h, w_out_bf16, g, b, tm, alpha):
    rows, d = h.shape
    return pl.pallas_call(
        functools.partial(_out_proj_kernel, alpha=alpha),
        grid=(rows // tm,),
        in_specs=[
            pl.BlockSpec((tm, D_SB), lambda i: (i, 0)),
            pl.BlockSpec((tm, D_MB), lambda i: (i, 0)),
            pl.BlockSpec((tm, d), lambda i: (i, 0)),
            pl.BlockSpec((D_SB, d), lambda i: (0, 0)),
            pl.BlockSpec((D_MB, d), lambda i: (1, 0)),
            pl.BlockSpec((1, d), lambda i: (0, 0)),
            pl.BlockSpec((1, d), lambda i: (0, 0)),
        ],
        out_specs=[pl.BlockSpec((tm, d), lambda i: (i, 0)), pl.BlockSpec((tm, d), lambda i: (i, 0))],
        out_shape=[jax.ShapeDtypeStruct((rows, d), F32), jax.ShapeDtypeStruct((rows, d), BF16)],
        compiler_params=_cparams(("arbitrary",)),
        name="out_proj",
    )(o_sb, o_mb, h, w_out_bf16, w_out_bf16, g, b)


HALO = 16


def _conv_ffn_kernel(*refs, alpha, seq_len, tail, has_state):
    if has_state:
        (h1_ref, h1b_ref, halo_ref, wg_ref, wu_ref, wd_ref, cw_ref, cb_ref, g_ref, b_ref, p1_ref, p2_ref,
         out_ref, gt_ref, x_ref, gbuf_ref, acc_ref) = refs
    else:
        (h1_ref, h1b_ref, halo_ref, wg_ref, wu_ref, wd_ref, cw_ref, cb_ref, g_ref, b_ref,
         out_ref, gt_ref, x_ref, gbuf_ref, acc_ref) = refs
    i = pl.program_id(0)
    j = pl.program_id(1)
    tm = h1_ref.shape[0]

    @pl.when(j == 0)
    def _():
        x_ref[0:HALO, :] = halo_ref[...]
        x_ref[HALO:HALO + tm, :] = h1b_ref[...]
        acc_ref[...] = jnp.zeros(acc_ref.shape, F32)

    gbuf_ref[...] = _dot(x_ref[...], wg_ref[...])
    u = _dot(x_ref[HALO:HALO + tm, :], wu_ref[...])
    g0 = gbuf_ref[HALO:HALO + tm, :]
    g1 = gbuf_ref[HALO - 1:HALO - 1 + tm, :]
    g2 = gbuf_ref[HALO - 2:HALO - 2 + tm, :]
    pos = (i * tm + lax.broadcasted_iota(jnp.int32, (tm, 1), 0)) % seq_len
    g1 = jnp.where(pos >= 1, g1, p1_ref[...] if has_state else 0.0)
    g2 = jnp.where(pos >= 2, g2, p2_ref[...] if has_state else 0.0)
    cw = cw_ref[...]
    c = cb_ref[...] + cw[0:1, :] * g2 + cw[1:2, :] * g1 + cw[2:3, :] * g0
    a = 0.5 * c * (1.0 + lax.erf(c * math.sqrt(0.5))) * u
    acc_ref[...] += _dot(a.astype(BF16), wd_ref[...])
    gt_ref[0] = g0[tm - tail:, :]

    @pl.when(j == pl.num_programs(1) - 1)
    def _():
        out_ref[...] = _layer_norm(alpha * h1_ref[...] + acc_ref[...], g_ref[...], b_ref[...])


def _conv_ffn(h1, h1b, wg, wu, wd, conv_w, conv_b, g, b, tm, tf, seq_len, tail, alpha, state=None):
    rows, d = h1.shape
    f = wg.shape[1]
    has_state = state is not None
    in_specs = [
        pl.BlockSpec((tm, d), lambda i, j: (i, 0)),
        pl.BlockSpec((tm, d), lambda i, j: (i, 0)),
        pl.BlockSpec((HALO, d), lambda i, j: (jnp.maximum(i * (tm // HALO) - 1, 0), 0)),
        pl.BlockSpec((d, tf), lambda i, j: (0, j)),
        pl.BlockSpec((d, tf), lambda i, j: (0, j)),
        pl.BlockSpec((tf, d), lambda i, j: (j, 0)),
        pl.BlockSpec((CONV_WIDTH, tf), lambda i, j: (0, j)),
        pl.BlockSpec((1, tf), lambda i, j: (0, j)),
        pl.BlockSpec((1, d), lambda i, j: (0, 0)),
        pl.BlockSpec((1, d), lambda i, j: (0, 0)),
    ]
    args = [h1, h1b, h1b, wg, wu, wd, conv_w, conv_b, g, b]
    if has_state:
        in_specs += [pl.BlockSpec((tm, tf), lambda i, j: (i, j)), pl.BlockSpec((tm, tf), lambda i, j: (i, j))]
        args += list(state)
    return pl.pallas_call(
        functools.partial(_conv_ffn_kernel, alpha=alpha, seq_len=seq_len, tail=tail, has_state=has_state),
        grid=(rows // tm, f // tf),
        in_specs=in_specs,
        out_specs=[pl.BlockSpec((tm, d), lambda i, j: (i, 0)),
                   pl.BlockSpec((1, tail, tf), lambda i, j: (i, 0, j))],
        out_shape=[jax.ShapeDtypeStruct((rows, d), F32),
                   jax.ShapeDtypeStruct((rows // tm, tail, f), F32)],
        scratch_shapes=[pltpu.VMEM((HALO + tm, d), BF16), pltpu.VMEM((HALO + tm, tf), F32),
                        pltpu.VMEM((tm, d), F32)],
        compiler_params=_cparams(("arbitrary", "arbitrary")),
        name="conv_ffn",
    )(*args)


def _rope_tables(pos):
    half = HEAD_DIM // 2
    inv = ROPE_THETA ** (-jnp.arange(half, dtype=F32) / half)
    ang = pos.astype(F32)[:, None] * inv[None, :]
    cos, sin = jnp.cos(ang), jnp.sin(ang)
    return jnp.concatenate([cos, cos], axis=-1), jnp.concatenate([-sin, sin], axis=-1)


def _row(v):
    return v.reshape(1, -1)


def kernel(x_prompt, x_sample, cache_kv_sb, cache_kv_mb, state_conv, page_table, ln_in_g, ln_in_b, w_in, g_sb, g_mb, w_out, ln1_g, ln1_b, w_gate, w_up, conv_w, conv_b, w_down, ln2_g, ln2_b):
    depth = w_in.shape[0]
    n_p, t_p, d = x_prompt.shape
    n_s, t_s, _ = x_sample.shape
    f = w_gate.shape[-1]
    past = page_table.shape[1] * PAGE_SIZE
    alpha = (2 * depth) ** 0.25
    assert depth == 1, "the input LayerNorm is fused into the first layer's projection"
    assert t_p % MOBA_BLOCK == 0 and past % MOBA_BLOCK == 0
    l = 0

    tm_p, tf = 512, 512
    rows_p, rows_s = n_p * t_p, n_s * t_s
    cos_p, sin_p = _rope_tables(jnp.arange(t_p, dtype=jnp.int32))
    cos_s, sin_s = _rope_tables(past + jnp.arange(t_s, dtype=jnp.int32))
    cos_s, sin_s = jnp.tile(cos_s, (n_s, 1)), jnp.tile(sin_s, (n_s, 1))

    ln_g, ln_b = _row(ln_in_g), _row(ln_in_b)
    w_in_b = w_in[l].astype(BF16)
    w_out_b = w_out[l].astype(BF16)
    wg_b, wu_b, wd_b = w_gate[l].astype(BF16), w_up[l].astype(BF16), w_down[l].astype(BF16)
    ffn_consts = (conv_w[l], _row(conv_b[l]), _row(ln2_g[l]), _row(ln2_b[l]))

    h, qsb, kvsb, qmb, kvmb, kvsb_bf, kvmb_bf, kmean = _ln_qkv(
        x_prompt.reshape(rows_p, d), ln_g, ln_b, cos_p, sin_p, w_in_b, tm_p, t_p // tm_p, True)
    o_sb = _sb_prompt(qsb, kvsb_bf, _row(g_sb[l]), n_p, t_p)
    o_mb = _moba_prompt(qmb, kvmb_bf, kmean.reshape(n_p, t_p // MOBA_BLOCK, D_MB), _row(g_mb[l]), n_p, t_p)
    h1, h1b = _out_proj(o_sb, o_mb, h, w_out_b, _row(ln1_g[l]), _row(ln1_b[l]), tm_p, alpha)
    y_p, g_tail = _conv_ffn(h1, h1b, wg_b, wu_b, wd_b, *ffn_consts, tm_p, tf, t_p, 8, alpha)
    kv_sb_p = kvsb.reshape(n_p, t_p, 2, N_HEADS_SB, HEAD_DIM)
    kv_mb_p = kvmb.reshape(n_p, t_p, 2, N_HEADS_MB, HEAD_DIM)
    tiles_per_seq = t_p // tm_p
    conv_p = g_tail[tiles_per_seq - 1::tiles_per_seq, 8 - (CONV_WIDTH - 1):, :]

    h, qsb, kvsb, qmb, kvmb = _ln_qkv(x_sample.reshape(rows_s, d), ln_g, ln_b, cos_s, sin_s, w_in_b, rows_s, 1,
                                      False)
    kv_sb_s = kvsb.reshape(n_s, t_s, 2, N_HEADS_SB, HEAD_DIM)
    kv_mb_s = kvmb.reshape(n_s, t_s, 2, N_HEADS_MB, HEAD_DIM)
    o_sb = _sb_sample(page_table, qsb.reshape(n_s, t_s * N_HEADS_SB, HEAD_DIM),
                      kv_sb_s[:, :, 0].reshape(n_s, t_s * N_HEADS_SB, HEAD_DIM),
                      kv_sb_s[:, :, 1].reshape(n_s, t_s * N_HEADS_SB, HEAD_DIM),
                      jnp.tile(g_sb[l].reshape(N_HEADS_SB, HEAD_DIM), (t_s, 1)), cache_kv_sb[l])
    o_mb = _moba_sample(page_table, qmb.reshape(n_s, t_s * N_HEADS_MB, HEAD_DIM),
                        kv_mb_s[:, :, 0].reshape(n_s, t_s * N_HEADS_MB, HEAD_DIM),
                        kv_mb_s[:, :, 1].reshape(n_s, t_s * N_HEADS_MB, HEAD_DIM),
                        jnp.tile(g_mb[l].reshape(N_HEADS_MB, HEAD_DIM), (t_s, 1)), cache_kv_mb[l])
    h1, h1b = _out_proj(o_sb.reshape(rows_s, D_SB), o_mb.reshape(rows_s, D_MB), h, w_out_b,
                        _row(ln1_g[l]), _row(ln1_b[l]), rows_s, alpha)
    sc = state_conv[l].astype(F32)
    zero = jnp.zeros((n_s, t_s, f), F32)
    prev1 = zero.at[:, 0].set(sc[:, 1]).reshape(rows_s, f)
    prev2 = zero.at[:, 0].set(sc[:, 0]).at[:, 1].set(sc[:, 1]).reshape(rows_s, f)
    y_s, g_all = _conv_ffn(h1, h1b, wg_b, wu_b, wd_b, *ffn_consts, rows_s, tf, t_s, rows_s, alpha,
                           state=(prev1, prev2))
    conv_s = g_all.reshape(n_s, t_s, f)[:, t_s - (CONV_WIDTH - 1):]

    return (y_p.reshape(n_p, t_p, d), y_s.reshape(n_s, t_s, d), kv_sb_p[None], kv_mb_p[None], conv_p[None],
            kv_sb_s[None], kv_mb_s[None], conv_s[None])
```

```python
import functools
import math

import jax
import jax.numpy as jnp
from jax import lax
from jax.experimental import pallas as pl
from jax.experimental.pallas import tpu as pltpu

F32 = jnp.float32
BF16 = jnp.bfloat16

HEAD_DIM = 128
N_HEADS_SB = 8
N_HEADS_MB = 8
D_SB = N_HEADS_SB * HEAD_DIM
D_MB = N_HEADS_MB * HEAD_DIM
MOBA_BLOCK = 256
MOBA_TOPK = 3
ROPE_THETA = 10000.0
CONV_WIDTH = 3
PAGE_SIZE = 128
LN_EPS = 1e-5
RMS_EPS = 1e-6
NEG = -1e30
SCALE = 1.0 / math.sqrt(HEAD_DIM)

SB_SKIP_BELOW = -105.0

SB_TQ = 128
LANES = 128
VMEM_LIMIT_V7X = 56 * 1024 * 1024


def _cparams(sem):
    return pltpu.CompilerParams(dimension_semantics=sem, vmem_limit_bytes=VMEM_LIMIT_V7X)


def _layer_norm(x, g, b):
    mu = jnp.mean(x, axis=-1, keepdims=True)
    xc = x - mu
    var = jnp.mean(xc * xc, axis=-1, keepdims=True)
    return xc * lax.rsqrt(var + LN_EPS) * g + b


def _head_rms(o, g):
    ms = jnp.mean(o * o, axis=-1, keepdims=True)
    return o * lax.rsqrt(ms + RMS_EPS) * g


def _dot_nt(a, b):
    return lax.dot_general(a, b, (((1,), (1,)), ((), ())), preferred_element_type=F32)


def _dot(a, b):
    return jnp.dot(a, b, preferred_element_type=F32)


def _split_bf16(x):
    hi = x.astype(BF16)
    lo = (x - hi.astype(F32)).astype(BF16)
    return hi, lo


def _softplus(z):
    return jnp.maximum(z, 0.0) + jnp.log1p(jnp.exp(-jnp.abs(z)))


def _tri(n):
    r = lax.broadcasted_iota(jnp.int32, (n, n), 0)
    s = lax.broadcasted_iota(jnp.int32, (n, n), 1)
    return (r > s).astype(BF16)


QKV_TN = 512
QKV_STEPS_PER_SEG = D_SB // QKV_TN

def _ln_qkv_kernel(x_ref, g_ref, b_ref, cos_ref, sin_ref, w_ref, h_ref, qsb_ref, kvsb_ref, qmb_ref, kvmb_ref,
                   *rest, attention_copies):
    if attention_copies:
        kvsb_bf_ref, kvmb_bf_ref, kmean_ref, hb_ref = rest
    else:
        (hb_ref,) = rest
    j = pl.program_id(1)

    @pl.when(j == 0)
    def _():
        h = _layer_norm(x_ref[...], g_ref[...], b_ref[...])
        h_ref[...] = h
        hb_ref[...] = h.astype(BF16)

    seg = j // QKV_STEPS_PER_SEG

    def project():
        return _dot(hb_ref[...], w_ref[...])

    def rope(a):
        cos = cos_ref[...]
        sin = sin_ref[...]
        heads = []
        for hh in range(a.shape[1] // HEAD_DIM):
            ah = a[:, hh * HEAD_DIM:(hh + 1) * HEAD_DIM]
            heads.append(ah * cos + pltpu.roll(ah, HEAD_DIM // 2, axis=1) * sin)
        return jnp.concatenate(heads, axis=1)

    @pl.when(seg == 0)
    def _():
        qsb_ref[...] = project().astype(BF16)

    @pl.when((seg == 1) | (seg == 2))
    def _():
        acc = project()
        kvsb_ref[...] = acc
        if attention_copies:
            kvsb_bf_ref[...] = acc.astype(BF16)

    @pl.when(seg == 3)
    def _():
        qmb_ref[...] = rope(project())

    @pl.when(seg == 4)
    def _():
        k = rope(project())
        kvmb_ref[...] = k
        if attention_copies:
            kvmb_bf_ref[...] = k.astype(BF16)
            for r in range(k.shape[0] // MOBA_BLOCK):
                kmean_ref[0, r:r + 1, :] = jnp.mean(k[r * MOBA_BLOCK:(r + 1) * MOBA_BLOCK, :], axis=0,
                                                    keepdims=True)

    @pl.when(seg == 5)
    def _():
        acc = project()
        kvmb_ref[...] = acc
        if attention_copies:
            kvmb_bf_ref[...] = acc.astype(BF16)


def _ln_qkv(x, g, b, cos, sin, w_bf16, tm, n_pos_tiles, attention_copies):
    rows, d = x.shape
    assert w_bf16.shape[1] == 3 * D_SB + 3 * D_MB and D_SB == D_MB
    sps = QKV_STEPS_PER_SEG
    grid = (rows // tm, 6 * sps)
    col = lambda j, first_seg, n_seg: jnp.minimum(jnp.maximum(j - first_seg * sps, 0), n_seg * sps - 1)
    out_specs = [
        pl.BlockSpec((tm, d), lambda i, j: (i, 0)),
        pl.BlockSpec((tm, QKV_TN), lambda i, j: (i, col(j, 0, 1))),
        pl.BlockSpec((tm, QKV_TN), lambda i, j: (i, col(j, 1, 2))),
        pl.BlockSpec((tm, QKV_TN), lambda i, j: (i, col(j, 3, 1))),
        pl.BlockSpec((tm, QKV_TN), lambda i, j: (i, col(j, 4, 2))),
    ]
    out_shape = [
        jax.ShapeDtypeStruct((rows, d), F32),
        jax.ShapeDtypeStruct((rows, D_SB), BF16),
        jax.ShapeDtypeStruct((rows, 2 * D_SB), F32),
        jax.ShapeDtypeStruct((rows, D_MB), F32),
        jax.ShapeDtypeStruct((rows, 2 * D_MB), F32),
    ]
    if attention_copies:
        assert tm % MOBA_BLOCK == 0
        bpt = tm // MOBA_BLOCK
        out_specs += [
            pl.BlockSpec((tm, QKV_TN), lambda i, j: (i, col(j, 1, 2))),
            pl.BlockSpec((tm, QKV_TN), lambda i, j: (i, col(j, 4, 2))),
            pl.BlockSpec((1, bpt, QKV_TN), lambda i, j: (i, 0, col(j, 4, 1))),
        ]
        out_shape += [
            jax.ShapeDtypeStruct((rows, 2 * D_SB), BF16),
            jax.ShapeDtypeStruct((rows, 2 * D_MB), BF16),
            jax.ShapeDtypeStruct((rows // tm, bpt, D_MB), F32),
        ]
    return pl.pallas_call(
        functools.partial(_ln_qkv_kernel, attention_copies=attention_copies),
        grid=grid,
        in_specs=[
            pl.BlockSpec((tm, d), lambda i, j: (i, 0)),
            pl.BlockSpec((1, d), lambda i, j: (0, 0)),
            pl.BlockSpec((1, d), lambda i, j: (0, 0)),
            pl.BlockSpec((tm, HEAD_DIM), lambda i, j: (i % n_pos_tiles, 0)),
            pl.BlockSpec((tm, HEAD_DIM), lambda i, j: (i % n_pos_tiles, 0)),
            pl.BlockSpec((d, QKV_TN), lambda i, j: (0, j)),
        ],
        out_specs=out_specs,
        out_shape=out_shape,
        scratch_shapes=[pltpu.VMEM((tm, d), BF16)],
        compiler_params=_cparams(("arbitrary", "arbitrary")),
        name="ln_qkv",
    )(x, g, b, cos, sin, w_bf16)


def _sb_prompt_kernel(q_ref, kv_ref, g_ref, o_ref, c_ref, acc_ref):
    qi = pl.program_id(1)
    nh, tq = N_HEADS_SB, SB_TQ
    tri = _tri(tq)
    row = lax.broadcasted_iota(jnp.int32, (nh * tq, tq), 0) % tq
    col = lax.broadcasted_iota(jnp.int32, (nh * tq, tq), 1)

    def head(h):
        return slice(h * HEAD_DIM, (h + 1) * HEAD_DIM)

    def attend(kb, valid):
        keys = pl.ds(pl.multiple_of(kb * tq, tq), tq)
        z = jnp.concatenate([_dot_nt(q_ref[:, head(h)], kv_ref[keys, head(h)]) for h in range(nh)],
                            axis=0) * SCALE
        sp = _softplus(z)
        l1mb = -sp if valid is None else jnp.where(valid, -sp, 0.0)
        hi, lo = _split_bf16(l1mb)
        c = c_ref[...]
        after = _dot(hi, tri) + _dot(lo, tri) + c
        w = jnp.exp(z - sp + after)
        if valid is not None:
            w = jnp.where(valid, w, 0.0)
        wb = w.astype(BF16)
        acc_ref[...] += jnp.concatenate(
            [_dot(wb[h * tq:(h + 1) * tq, :], kv_ref[keys, head(nh + h)]) for h in range(nh)], axis=0)
        c = c + jnp.sum(l1mb, axis=-1, keepdims=True)
        c_ref[...] = c
        return jnp.max(c)

    c_ref[...] = jnp.zeros(c_ref.shape, F32)
    acc_ref[...] = jnp.zeros(acc_ref.shape, F32)
    c_max = attend(qi, col < row)

    def cond(st):
        kb, c_max = st
        return (kb >= 0) & (c_max > SB_SKIP_BELOW)

    def body(st):
        kb, _ = st
        return kb - 1, attend(kb, None)

    lax.while_loop(cond, body, (qi - 1, c_max))
    for h in range(nh):
        o_ref[:, head(h)] = _head_rms(acc_ref[h * tq:(h + 1) * tq, :], g_ref[:, head(h)]).astype(BF16)


def _sb_prompt(q, kv_bf16, g, n_batch, t):
    rows = q.shape[0]
    nq = t // SB_TQ
    return pl.pallas_call(
        _sb_prompt_kernel,
        grid=(n_batch, nq),
        in_specs=[
            pl.BlockSpec((SB_TQ, D_SB), lambda b, i: (b * nq + i, 0)),
            pl.BlockSpec((t, 2 * D_SB), lambda b, i: (b, 0)),
            pl.BlockSpec((1, D_SB), lambda b, i: (0, 0)),
        ],
        out_specs=pl.BlockSpec((SB_TQ, D_SB), lambda b, i: (b * nq + i, 0)),
        out_shape=jax.ShapeDtypeStruct((rows, D_SB), BF16),
        scratch_shapes=[pltpu.VMEM((N_HEADS_SB * SB_TQ, 1), F32),
                        pltpu.VMEM((N_HEADS_SB * SB_TQ, HEAD_DIM), F32)],
        compiler_params=_cparams(("arbitrary", "arbitrary")),
        name="sb_prompt",
    )(q, kv_bf16, g)


def _topk_mask(gate, n_valid):
    col = lax.broadcasted_iota(jnp.int32, gate.shape, 1)
    colf = col.astype(F32)
    g = jnp.where(col < n_valid, gate, NEG)
    sel = jnp.zeros(gate.shape, F32)
    for _ in range(MOBA_TOPK):
        m = jnp.max(g, axis=-1, keepdims=True)
        first = jnp.min(jnp.where(g == m, colf, float(gate.shape[1])), axis=-1, keepdims=True)
        pick = (colf == first) & (m > 0.5 * NEG)
        sel = jnp.where(pick, 1.0, sel)
        g = jnp.where(pick, NEG, g)
    return sel


MOBA_CHUNK = 4
MOBA_HEADS_PER_STEP = 4
LOG2E = 1.4426950408889634


def _moba_prompt_kernel(q_ref, k_ref, v_ref, kmean_ref, g_ref, o_ref,
                        km_ref, kaug_ref, qaug_ref, m_ref, l_ref, acc_ref, *, nb):
    qi = pl.program_id(2)
    tq = MOBA_BLOCK
    t = k_ref.shape[0]
    heads = range(MOBA_HEADS_PER_STEP)

    def head(h):
        return slice(h * HEAD_DIM, (h + 1) * HEAD_DIM)

    @pl.when(qi == 0)
    def _():
        km_ref[...] = jnp.zeros(km_ref.shape, F32)
        blk = lax.broadcasted_iota(jnp.int32, (t, LANES), 0) // tq
        onehot = (blk == lax.broadcasted_iota(jnp.int32, (t, LANES), 1)).astype(BF16)
        for h in heads:
            km_ref[h, 0:nb, :] = kmean_ref[0, :, head(h)]
            kaug_ref[h, :, 0:HEAD_DIM] = k_ref[:, head(h)]
            kaug_ref[h, :, HEAD_DIM:HEAD_DIM + LANES] = onehot

    row = lax.broadcasted_iota(jnp.int32, (tq, tq), 0)
    col = lax.broadcasted_iota(jnp.int32, (tq, tq), 1)
    own = pl.ds(pl.multiple_of(qi * tq, tq), tq)

    for h in heads:
        qh = q_ref[:, head(h)].astype(BF16)
        gate = _dot_nt(qh, km_ref[h].astype(BF16))
        sel = _topk_mask(gate, qi)
        qaug_ref[h, :, 0:HEAD_DIM] = qh
        qaug_ref[h, :, HEAD_DIM:HEAD_DIM + LANES] = jnp.where(sel > 0.5, 0.0, NEG).astype(BF16)

        s = jnp.where(col <= row, _dot_nt(qh, k_ref[own, head(h)]) * (SCALE * LOG2E), NEG)
        m = jnp.max(s, axis=-1, keepdims=True)
        p = jnp.exp2(s - m)
        m_ref[h] = m
        l_ref[h] = jnp.sum(p, axis=-1, keepdims=True)
        acc_ref[h] = _dot(p.astype(BF16), v_ref[own, head(h)])

    def body(j, carry):
        keys = pl.ds(pl.multiple_of(j * (MOBA_CHUNK * tq), MOBA_CHUNK * tq), MOBA_CHUNK * tq)
        for h in heads:
            s = _dot_nt(qaug_ref[h], kaug_ref[h, keys, :]) * (SCALE * LOG2E)
            m = m_ref[h]
            m_new = jnp.maximum(m, jnp.max(s, axis=-1, keepdims=True))
            alpha = jnp.exp2(m - m_new)
            p = jnp.exp2(s - m_new)
            m_ref[h] = m_new
            l_ref[h] = alpha * l_ref[h] + jnp.sum(p, axis=-1, keepdims=True)
            acc_ref[h] = alpha * acc_ref[h] + _dot(p.astype(BF16), v_ref[keys, head(h)])
        return carry

    lax.fori_loop(0, (qi + MOBA_CHUNK - 1) // MOBA_CHUNK, body, 0)
    for h in heads:
        o_ref[:, head(h)] = _head_rms(acc_ref[h] / l_ref[h], g_ref[:, head(h)]).astype(BF16)


def _moba_prompt(q, kv_bf16, kmean, g, n_batch, t):
    rows = q.shape[0]
    tq = MOBA_BLOCK
    nb = t // tq
    hps = MOBA_HEADS_PER_STEP
    groups = N_HEADS_MB // hps
    width = hps * HEAD_DIM
    assert nb <= LANES and nb % MOBA_CHUNK == 0 and N_HEADS_MB % hps == 0
    return pl.pallas_call(
        functools.partial(_moba_prompt_kernel, nb=nb),
        grid=(n_batch, groups, nb),
        in_specs=[
            pl.BlockSpec((tq, width), lambda b, h, i: (b * nb + i, h)),
            pl.BlockSpec((t, width), lambda b, h, i: (b, h)),
            pl.BlockSpec((t, width), lambda b, h, i: (b, groups + h)),
            pl.BlockSpec((1, nb, width), lambda b, h, i: (b, 0, h)),
            pl.BlockSpec((1, width), lambda b, h, i: (0, h)),
        ],
        out_specs=pl.BlockSpec((tq, width), lambda b, h, i: (b * nb + i, h)),
        out_shape=jax.ShapeDtypeStruct((rows, D_MB), BF16),
        scratch_shapes=[pltpu.VMEM((hps, LANES, HEAD_DIM), F32),
                        pltpu.VMEM((hps, t, HEAD_DIM + LANES), BF16),
                        pltpu.VMEM((hps, tq, HEAD_DIM + LANES), BF16),
                        pltpu.VMEM((hps, tq, 1), F32),
                        pltpu.VMEM((hps, tq, 1), F32),
                        pltpu.VMEM((hps, tq, HEAD_DIM), F32)],
        compiler_params=_cparams(("arbitrary", "arbitrary", "arbitrary")),
        name="moba_prompt",
    )(q, kv_bf16, kv_bf16, kmean, g)


def _head_match(m_rows, n_cols, nh):
    r = lax.broadcasted_iota(jnp.int32, (m_rows, n_cols), 0)
    c = lax.broadcasted_iota(jnp.int32, (m_rows, n_cols), 1)
    return (r % nh) == (c % nh), r // nh, c // nh


def _sb_rows(q, k_rows, v_rows, c, acc, tri, valid):
    m_rows, n = q.shape[0], k_rows.shape[0]
    z = _dot_nt(q, k_rows) * SCALE
    sp = _softplus(z)
    l1mb = jnp.where(valid, -sp, 0.0)
    tiles = [l1mb[:, j * LANES:(j + 1) * LANES] for j in range(n // LANES)]
    later = [None] * len(tiles)
    for j in reversed(range(len(tiles))):
        later[j] = c
        c = c + jnp.sum(tiles[j], axis=-1, keepdims=True)
    hi, lo = _split_bf16(jnp.concatenate(tiles, axis=0))
    within = _dot(hi, tri) + _dot(lo, tri)
    after = jnp.concatenate(
        [within[j * m_rows:(j + 1) * m_rows] + later[j] for j in range(len(tiles))], axis=1)
    w = jnp.where(valid, jnp.exp(z - sp + after), 0.0)
    return c, acc + _dot(w.astype(BF16), v_rows)


def _sb_sample_kernel(pt_ref, q_ref, kn_ref, vn_ref, g_ref, cache_ref, o_ref,
                      buf_ref, newk_ref, newv_ref, sem, *, n_pages):
    b = pl.program_id(0)
    nh = N_HEADS_SB
    m_rows = q_ref.shape[1]
    n_new = kn_ref.shape[1]
    page_rows = PAGE_SIZE * nh

    def page_copy(p, slot):
        return pltpu.make_async_copy(cache_ref.at[pt_ref[b, p]], buf_ref.at[slot], sem.at[slot])

    page_copy(n_pages - 1, 0).start()

    newk_ref[...] = jnp.zeros(newk_ref.shape, F32)
    newv_ref[...] = jnp.zeros(newv_ref.shape, F32)
    newk_ref[0:n_new, :] = kn_ref[0]
    newv_ref[0:n_new, :] = vn_ref[0]

    q = q_ref[0]
    tri = _tri(LANES)

    match, qt, kt = _head_match(m_rows, LANES, nh)
    c, acc = _sb_rows(q, newk_ref[...].astype(BF16), newv_ref[...].astype(BF16),
                      jnp.zeros((m_rows, 1), F32), jnp.zeros((m_rows, HEAD_DIM), F32),
                      tri, match & (kt < qt))
    page_match, _, _ = _head_match(m_rows, page_rows, nh)

    def cond(st):
        p, c, _ = st
        return (p >= 0) & (jnp.max(c) > SB_SKIP_BELOW)

    def body(st):
        p, c, acc = st
        slot = (n_pages - 1 - p) % 2
        page_copy(p, slot).wait()

        @pl.when(p >= 1)
        def _():
            page_copy(p - 1, 1 - slot).start()

        k_rows = buf_ref[slot, :, 0].reshape(page_rows, HEAD_DIM).astype(BF16)
        v_rows = buf_ref[slot, :, 1].reshape(page_rows, HEAD_DIM).astype(BF16)
        c, acc = _sb_rows(q, k_rows, v_rows, c, acc, tri, page_match)
        return p - 1, c, acc

    p_end, _, acc = lax.while_loop(cond, body, (n_pages - 1, c, acc))

    @pl.when(p_end >= 0)
    def _():
        page_copy(p_end, (n_pages - 1 - p_end) % 2).wait()

    o_ref[0] = _head_rms(acc, g_ref[...]).astype(BF16)


def _sb_sample(page_table, q, k_new, v_new, g_rows, cache):
    nb, n_pages = page_table.shape
    m_rows, n_new = q.shape[1], k_new.shape[1]
    nh = cache.shape[3]
    assert n_new <= LANES and m_rows % 8 == 0
    grid_spec = pltpu.PrefetchScalarGridSpec(
        num_scalar_prefetch=1,
        grid=(nb,),
        in_specs=[
            pl.BlockSpec((1, m_rows, HEAD_DIM), lambda b, pt: (b, 0, 0)),
            pl.BlockSpec((1, n_new, HEAD_DIM), lambda b, pt: (b, 0, 0)),
            pl.BlockSpec((1, n_new, HEAD_DIM), lambda b, pt: (b, 0, 0)),
            pl.BlockSpec((m_rows, HEAD_DIM), lambda b, pt: (0, 0)),
            pl.BlockSpec(memory_space=pl.ANY),
        ],
        out_specs=pl.BlockSpec((1, m_rows, HEAD_DIM), lambda b, pt: (b, 0, 0)),
        scratch_shapes=[pltpu.VMEM((2, PAGE_SIZE, 2, nh, HEAD_DIM), F32),
                        pltpu.VMEM((LANES, HEAD_DIM), F32),
                        pltpu.VMEM((LANES, HEAD_DIM), F32),
                        pltpu.SemaphoreType.DMA((2,))],
    )
    return pl.pallas_call(
        functools.partial(_sb_sample_kernel, n_pages=n_pages),
        grid_spec=grid_spec,
        out_shape=jax.ShapeDtypeStruct((nb, m_rows, HEAD_DIM), BF16),
        compiler_params=_cparams(("arbitrary",)),
        name="sb_sample",
    )(page_table, q, k_new, v_new, g_rows, cache)


MOBA_PAGES_PER_STEP = 8


def _moba_sample_kernel(pt_ref, q_ref, kn_ref, vn_ref, g_ref, *rest, n_pages):
    page_refs = rest[:MOBA_PAGES_PER_STEP]
    o_ref, newk_ref, newv_ref, m_ref, l_ref, s_ref, oall_ref = rest[MOBA_PAGES_PER_STEP:]
    step = pl.program_id(1)
    nh = N_HEADS_MB
    m_rows = q_ref.shape[1]
    n_new = kn_ref.shape[1]
    page_rows = PAGE_SIZE * nh
    ppb = MOBA_BLOCK // PAGE_SIZE
    bps = MOBA_PAGES_PER_STEP // ppb

    qh, ql = _split_bf16(q_ref[0])
    qq = jnp.concatenate([qh, ql], axis=0)

    def partial_softmax(k_rows, v_rows, valid):
        qk2 = _dot_nt(qq, k_rows)
        qk = qk2[0:m_rows] + qk2[m_rows:2 * m_rows]
        s = jnp.where(valid, qk * SCALE, NEG)
        m = jnp.max(s, axis=-1, keepdims=True)
        e = jnp.exp(s - m)
        l = jnp.sum(e, axis=-1, keepdims=True)
        o = _dot(e.astype(BF16), v_rows)
        return jnp.sum(jnp.where(valid, qk, 0.0), axis=-1, keepdims=True), m, l, o

    lane = lax.broadcasted_iota(jnp.int32, (m_rows, LANES), 1)

    @pl.when(step == 0)
    def _():
        m_ref[...] = jnp.full(m_ref.shape, NEG, F32)
        l_ref[...] = jnp.zeros(l_ref.shape, F32)
        s_ref[...] = jnp.zeros(s_ref.shape, F32)

    block_match, _, _ = _head_match(m_rows, ppb * page_rows, nh)

    def block_rows(i, kv):
        return jnp.concatenate([r[0, :, kv].reshape(page_rows, HEAD_DIM).astype(BF16)
                                for r in page_refs[i * ppb:(i + 1) * ppb]], axis=0)

    qk2 = [_dot_nt(qq, block_rows(i, 0)) for i in range(bps)]
    es = []
    m_all, l_all, s_all = m_ref[...], l_ref[...], s_ref[...]
    for i in range(bps):
        qk = qk2[i][0:m_rows] + qk2[i][m_rows:2 * m_rows]
        s = jnp.where(block_match, qk * SCALE, NEG)
        m = jnp.max(s, axis=-1, keepdims=True)
        e = jnp.exp(s - m)
        es.append(e.astype(BF16))
        here = lane == step * bps + i
        m_all = jnp.where(here, m, m_all)
        l_all = jnp.where(here, jnp.sum(e, axis=-1, keepdims=True), l_all)
        s_all = jnp.where(here, jnp.sum(jnp.where(block_match, qk, 0.0), axis=-1, keepdims=True), s_all)
    m_ref[...], l_ref[...], s_ref[...] = m_all, l_all, s_all
    for i in range(bps):
        oall_ref[step * bps + i] = _dot(es[i], block_rows(i, 1))

    @pl.when(step == pl.num_programs(1) - 1)
    def _():
        n_blocks = n_pages // ppb
        sel = _topk_mask(s_ref[...] * (1.0 / MOBA_BLOCK), n_blocks)

        newk_ref[...] = jnp.zeros(newk_ref.shape, F32)
        newv_ref[...] = jnp.zeros(newv_ref.shape, F32)
        newk_ref[0:n_new, :] = kn_ref[0]
        newv_ref[0:n_new, :] = vn_ref[0]
        match, qt, kt = _head_match(m_rows, LANES, nh)
        _, m_own, l_own, o_own = partial_softmax(newk_ref[...].astype(BF16), newv_ref[...].astype(BF16),
                                                 match & (kt <= qt))

        m_all = m_ref[...]
        m_tot = jnp.maximum(jnp.max(jnp.where(sel > 0.5, m_all, NEG), axis=-1, keepdims=True), m_own)
        wgt = jnp.where(sel > 0.5, jnp.exp(m_all - m_tot), 0.0)
        w_own = jnp.exp(m_own - m_tot)
        l_tot = jnp.sum(wgt * l_ref[...], axis=-1, keepdims=True) + w_own * l_own

        def merge(n, acc):
            w = jnp.sum(jnp.where(lane == n, wgt, 0.0), axis=-1, keepdims=True)
            return acc + w * oall_ref[n]

        acc = lax.fori_loop(0, n_blocks, merge, w_own * o_own)
        o_ref[0] = _head_rms(acc / l_tot, g_ref[...]).astype(BF16)


def _moba_sample(page_table, q, k_new, v_new, g_rows, cache):
    nb, n_pages = page_table.shape
    m_rows, n_new = q.shape[1], k_new.shape[1]
    nh = cache.shape[3]
    pps = MOBA_PAGES_PER_STEP
    assert n_pages <= LANES and MOBA_BLOCK == 2 * PAGE_SIZE and n_new <= LANES
    assert pps % 2 == 0 and n_pages % pps == 0
    page_spec = lambda i: pl.BlockSpec((1, PAGE_SIZE, 2, nh, HEAD_DIM),
                                       lambda b, s, pt: (pt[b, s * pps + i], 0, 0, 0, 0))
    grid_spec = pltpu.PrefetchScalarGridSpec(
        num_scalar_prefetch=1,
        grid=(nb, n_pages // pps),
        in_specs=[
            pl.BlockSpec((1, m_rows, HEAD_DIM), lambda b, s, pt: (b, 0, 0)),
            pl.BlockSpec((1, n_new, HEAD_DIM), lambda b, s, pt: (b, 0, 0)),
            pl.BlockSpec((1, n_new, HEAD_DIM), lambda b, s, pt: (b, 0, 0)),
            pl.BlockSpec((m_rows, HEAD_DIM), lambda b, s, pt: (0, 0)),
        ] + [page_spec(i) for i in range(pps)],
        out_specs=pl.BlockSpec((1, m_rows, HEAD_DIM), lambda b, s, pt: (b, 0, 0)),
        scratch_shapes=[pltpu.VMEM((LANES, HEAD_DIM), F32),
                        pltpu.VMEM((LANES, HEAD_DIM), F32),
                        pltpu.VMEM((m_rows, LANES), F32),
                        pltpu.VMEM((m_rows, LANES), F32),
                        pltpu.VMEM((m_rows, LANES), F32),
                        pltpu.VMEM((n_pages * PAGE_SIZE // MOBA_BLOCK, m_rows, HEAD_DIM), F32)],
    )
    return pl.pallas_call(
        functools.partial(_moba_sample_kernel, n_pages=n_pages),
        grid_spec=grid_spec,
        out_shape=jax.ShapeDtypeStruct((nb, m_rows, HEAD_DIM), BF16),
        compiler_params=_cparams(("arbitrary", "arbitrary")),
        name="moba_sample",
    )(page_table, q, k_new, v_new, g_rows, *([cache] * pps))


def _out_proj_kernel(osb_ref, omb_ref, h_ref, wa_ref, wb_ref, g_ref, b_ref, h1_ref, h1b_ref, *, alpha):
    mix = _dot(osb_ref[...], wa_ref[...]) + _dot(omb_ref[...], wb_ref[...])
    h1 = _layer_norm(alpha * h_ref[...] + mix, g_ref[...], b_ref[...])
    h1_ref[...] = h1
    h1b_ref[...] = h1.astype(BF16)


def _out_proj(o_sb, o_mb, h, w_out_bf16, g, b, tm, alpha):
    rows, d = h.shape
    return pl.pallas_call(
        functools.partial(_out_proj_kernel, alpha=alpha),
        grid=(rows // tm,),
        in_specs=[
            pl.BlockSpec((tm, D_SB), lambda i: (i, 0)),
            pl.BlockSpec((tm, D_MB), lambda i: (i, 0)),
            pl.BlockSpec((tm, d), lambda i: (i, 0)),
            pl.BlockSpec((D_SB, d), lambda i: (0, 0)),
            pl.BlockSpec((D_MB, d), lambda i: (1, 0)),
            pl.BlockSpec((1, d), lambda i: (0, 0)),
            pl.BlockSpec((1, d), lambda i: (0, 0)),
        ],
        out_specs=[pl.BlockSpec((tm, d), lambda i: (i, 0)), pl.BlockSpec((tm, d), lambda i: (i, 0))],
        out_shape=[jax.ShapeDtypeStruct((rows, d), F32), jax.ShapeDtypeStruct((rows, d), BF16)],
        compiler_params=_cparams(("arbitrary",)),
        name="out_proj",
    )(o_sb, o_mb, h, w_out_bf16, w_out_bf16, g, b)


HALO = 16


def _conv_ffn_kernel(*refs, alpha, seq_len, tail, has_state):
    if has_state:
        (h1_ref, h1b_ref, halo_ref, wg_ref, wu_ref, wd_ref, cw_ref, cb_ref, g_ref, b_ref, p1_ref, p2_ref,
         out_ref, gt_ref, x_ref, gbuf_ref, acc_ref) = refs
    else:
        (h1_ref, h1b_ref, halo_ref, wg_ref, wu_ref, wd_ref, cw_ref, cb_ref, g_ref, b_ref,
         out_ref, gt_ref, x_ref, gbuf_ref, acc_ref) = refs
    i = pl.program_id(0)
    j = pl.program_id(1)
    tm = h1_ref.shape[0]

    @pl.when(j == 0)
    def _():
        x_ref[0:HALO, :] = halo_ref[...]
        x_ref[HALO:HALO + tm, :] = h1b_ref[...]
        acc_ref[...] = jnp.zeros(acc_ref.shape, F32)

    gbuf_ref[...] = _dot(x_ref[...], wg_ref[...])
    u = _dot(x_ref[HALO:HALO + tm, :], wu_ref[...])
    g0 = gbuf_ref[HALO:HALO + tm, :]
    g1 = gbuf_ref[HALO - 1:HALO - 1 + tm, :]
    g2 = gbuf_ref[HALO - 2:HALO - 2 + tm, :]
    pos = (i * tm + lax.broadcasted_iota(jnp.int32, (tm, 1), 0)) % seq_len
    g1 = jnp.where(pos >= 1, g1, p1_ref[...] if has_state else 0.0)
    g2 = jnp.where(pos >= 2, g2, p2_ref[...] if has_state else 0.0)
    cw = cw_ref[...]
    c = cb_ref[...] + cw[0:1, :] * g2 + cw[1:2, :] * g1 + cw[2:3, :] * g0
    a = 0.5 * c * (1.0 + lax.erf(c * math.sqrt(0.5))) * u
    acc_ref[...] += _dot(a.astype(BF16), wd_ref[...])
    gt_ref[0] = g0[tm - tail:, :]

    @pl.when(j == pl.num_programs(1) - 1)
    def _():
        out_ref[...] = _layer_norm(alpha * h1_ref[...] + acc_ref[...], g_ref[...], b_ref[...])


def _conv_ffn(h1, h1b, wg, wu, wd, conv_w, conv_b, g, b, tm, tf, seq_len, tail, alpha, state=None):
    rows, d = h1.shape
    f = wg.shape[1]
    has_state = state is not None
    in_specs = [
        pl.BlockSpec((tm, d), lambda i, j: (i, 0)),
        pl.BlockSpec((tm, d), lambda i, j: (i, 0)),
        pl.BlockSpec((HALO, d), lambda i, j: (jnp.maximum(i * (tm // HALO) - 1, 0), 0)),
        pl.BlockSpec((d, tf), lambda i, j: (0, j)),
        pl.BlockSpec((d, tf), lambda i, j: (0, j)),
        pl.BlockSpec((tf, d), lambda i, j: (j, 0)),
        pl.BlockSpec((CONV_WIDTH, tf), lambda i, j: (0, j)),
        pl.BlockSpec((1, tf), lambda i, j: (0, j)),
        pl.BlockSpec((1, d), lambda i, j: (0, 0)),
        pl.BlockSpec((1, d), lambda i, j: (0, 0)),
    ]
    args = [h1, h1b, h1b, wg, wu, wd, conv_w, conv_b, g, b]
    if has_state:
        in_specs += [pl.BlockSpec((tm, tf), lambda i, j: (i, j)), pl.BlockSpec((tm, tf), lambda i, j: (i, j))]
        args += list(state)
    return pl.pallas_call(
        functools.partial(_conv_ffn_kernel, alpha=alpha, seq_len=seq_len, tail=tail, has_state=has_state),
        grid=(rows // tm, f // tf),
        in_specs=in_specs,
        out_specs=[pl.BlockSpec((tm, d), lambda i, j: (i, 0)),
                   pl.BlockSpec((1, tail, tf), lambda i, j: (i, 0, j))],
        out_shape=[jax.ShapeDtypeStruct((rows, d), F32),
                   jax.ShapeDtypeStruct((rows // tm, tail, f), F32)],
        scratch_shapes=[pltpu.VMEM((HALO + tm, d), BF16), pltpu.VMEM((HALO + tm, tf), F32),
                        pltpu.VMEM((tm, d), F32)],
        compiler_params=_cparams(("arbitrary", "arbitrary")),
        name="conv_ffn",
    )(*args)


def _rope_tables(pos):
    half = HEAD_DIM // 2
    inv = ROPE_THETA ** (-jnp.arange(half, dtype=F32) / half)
    ang = pos.astype(F32)[:, None] * inv[None, :]
    cos, sin = jnp.cos(ang), jnp.sin(ang)
    return jnp.concatenate([cos, cos], axis=-1), jnp.concatenate([-sin, sin], axis=-1)


def _row(v):
    return v.reshape(1, -1)


def kernel(x_prompt, x_sample, cache_kv_sb, cache_kv_mb, state_conv, page_table, ln_in_g, ln_in_b, w_in, g_sb, g_mb, w_out, ln1_g, ln1_b, w_gate, w_up, conv_w, conv_b, w_down, ln2_g, ln2_b):
    depth = w_in.shape[0]
    n_p, t_p, d = x_prompt.shape
    n_s, t_s, _ = x_sample.shape
    f = w_gate.shape[-1]
    past = page_table.shape[1] * PAGE_SIZE
    alpha = (2 * depth) ** 0.25
    assert depth == 1, "the input LayerNorm is fused into the first layer's projection"
    assert t_p % MOBA_BLOCK == 0 and past % MOBA_BLOCK == 0
    l = 0

    tm_p, tf = 512, 512
    rows_p, rows_s = n_p * t_p, n_s * t_s
    cos_p, sin_p = _rope_tables(jnp.arange(t_p, dtype=jnp.int32))
    cos_s, sin_s = _rope_tables(past + jnp.arange(t_s, dtype=jnp.int32))
    cos_s, sin_s = jnp.tile(cos_s, (n_s, 1)), jnp.tile(sin_s, (n_s, 1))

    ln_g, ln_b = _row(ln_in_g), _row(ln_in_b)
    w_in_b = w_in[l].astype(BF16)
    w_out_b = w_out[l].astype(BF16)
    wg_b, wu_b, wd_b = w_gate[l].astype(BF16), w_up[l].astype(BF16), w_down[l].astype(BF16)
    ffn_consts = (conv_w[l], _row(conv_b[l]), _row(ln2_g[l]), _row(ln2_b[l]))

    h, qsb, kvsb, qmb, kvmb, kvsb_bf, kvmb_bf, kmean = _ln_qkv(
        x_prompt.reshape(rows_p, d), ln_g, ln_b, cos_p, sin_p, w_in_b, tm_p, t_p // tm_p, True)
    o_sb = _sb_prompt(qsb, kvsb_bf, _row(g_sb[l]), n_p, t_p)
    o_mb = _moba_prompt(qmb, kvmb_bf, kmean.reshape(n_p, t_p // MOBA_BLOCK, D_MB), _row(g_mb[l]), n_p, t_p)
    h1, h1b = _out_proj(o_sb, o_mb, h, w_out_b, _row(ln1_g[l]), _row(ln1_b[l]), tm_p, alpha)
    y_p, g_tail = _conv_ffn(h1, h1b, wg_b, wu_b, wd_b, *ffn_consts, tm_p, tf, t_p, 8, alpha)
    kv_sb_p = kvsb.reshape(n_p, t_p, 2, N_HEADS_SB, HEAD_DIM)
    kv_mb_p = kvmb.reshape(n_p, t_p, 2, N_HEADS_MB, HEAD_DIM)
    tiles_per_seq = t_p // tm_p
    conv_p = g_tail[tiles_per_seq - 1::tiles_per_seq, 8 - (CONV_WIDTH - 1):, :]

    h, qsb, kvsb, qmb, kvmb = _ln_qkv(x_sample.reshape(rows_s, d), ln_g, ln_b, cos_s, sin_s, w_in_b, rows_s, 1,
                                      False)
    kv_sb_s = kvsb.reshape(n_s, t_s, 2, N_HEADS_SB, HEAD_DIM)
    kv_mb_s = kvmb.reshape(n_s, t_s, 2, N_HEADS_MB, HEAD_DIM)
    o_sb = _sb_sample(page_table, qsb.reshape(n_s, t_s * N_HEADS_SB, HEAD_DIM),
                      kv_sb_s[:, :, 0].reshape(n_s, t_s * N_HEADS_SB, HEAD_DIM),
                      kv_sb_s[:, :, 1].reshape(n_s, t_s * N_HEADS_SB, HEAD_DIM),
                      jnp.tile(g_sb[l].reshape(N_HEADS_SB, HEAD_DIM), (t_s, 1)), cache_kv_sb[l])
    o_mb = _moba_sample(page_table, qmb.reshape(n_s, t_s * N_HEADS_MB, HEAD_DIM),
                        kv_mb_s[:, :, 0].reshape(n_s, t_s * N_HEADS_MB, HEAD_DIM),
                        kv_mb_s[:, :, 1].reshape(n_s, t_s * N_HEADS_MB, HEAD_DIM),
                        jnp.tile(g_mb[l].reshape(N_HEADS_MB, HEAD_DIM), (t_s, 1)), cache_kv_mb[l])
    h1, h1b = _out_proj(o_sb.reshape(rows_s, D_SB), o_mb.reshape(rows_s, D_MB), h, w_out_b,
                        _row(ln1_g[l]), _row(ln1_b[l]), rows_s, alpha)
    sc = state_conv[l].astype(F32)
    zero = jnp.zeros((n_s, t_s, f), F32)
    prev1 = zero.at[:, 0].set(sc[:, 1]).reshape(rows_s, f)
    prev2 = zero.at[:, 0].set(sc[:, 0]).at[:, 1].set(sc[:, 1]).reshape(rows_s, f)
    y_s, g_all = _conv_ffn(h1, h1b, wg_b, wu_b, wd_b, *ffn_consts, rows_s, tf, t_s, rows_s, alpha,
                           state=(prev1, prev2))
    conv_s = g_all.reshape(n_s, t_s, f)[:, t_s - (CONV_WIDTH - 1):]

    return (y_p.reshape(n_p, t_p, d), y_s.reshape(n_s, t_s, d), kv_sb_p[None], kv_mb_p[None], conv_p[None],
            kv_sb_s[None], kv_mb_s[None], conv_s[None])
```

```python
import functools
import math

import jax
import jax.numpy as jnp
from jax import lax
from jax.experimental import pallas as pl
from jax.experimental.pallas import tpu as pltpu

F32 = jnp.float32
BF16 = jnp.bfloat16

HEAD_DIM = 128
N_HEADS_SB = 8
N_HEADS_MB = 8
D_SB = N_HEADS_SB * HEAD_DIM
D_MB = N_HEADS_MB * HEAD_DIM
MOBA_BLOCK = 256
MOBA_TOPK = 3
ROPE_THETA = 10000.0
CONV_WIDTH = 3
PAGE_SIZE = 128
LN_EPS = 1e-5
RMS_EPS = 1e-6
NEG = -1e30
SCALE = 1.0 / math.sqrt(HEAD_DIM)

SB_SKIP_BELOW = -105.0

SB_TQ = 128
LANES = 128
VMEM_LIMIT_V7X = 56 * 1024 * 1024


def _cparams(sem):
    return pltpu.CompilerParams(dimension_semantics=sem, vmem_limit_bytes=VMEM_LIMIT_V7X)


def _layer_norm(x, g, b):
    mu = jnp.mean(x, axis=-1, keepdims=True)
    xc = x - mu
    var = jnp.mean(xc * xc, axis=-1, keepdims=True)
    return xc * lax.rsqrt(var + LN_EPS) * g + b


def _head_rms(o, g):
    ms = jnp.mean(o * o, axis=-1, keepdims=True)
    return o * lax.rsqrt(ms + RMS_EPS) * g


def _dot_nt(a, b):
    return lax.dot_general(a, b, (((1,), (1,)), ((), ())), preferred_element_type=F32)


def _dot(a, b):
    return jnp.dot(a, b, preferred_element_type=F32)


def _split_bf16(x):
    hi = x.astype(BF16)
    lo = (x - hi.astype(F32)).astype(BF16)
    return hi, lo


def _softplus(z):
    return jnp.maximum(z, 0.0) + jnp.log(1.0 + jnp.exp(-jnp.abs(z)))


def _tri(n):
    r = lax.broadcasted_iota(jnp.int32, (n, n), 0)
    s = lax.broadcasted_iota(jnp.int32, (n, n), 1)
    return (r > s).astype(BF16)


QKV_TN = 512
QKV_STEPS_PER_SEG = D_SB // QKV_TN

def _ln_qkv_kernel(x_ref, g_ref, b_ref, cos_ref, sin_ref, w_ref, qsb_ref, kvsb_ref, qmb_ref, kvmb_ref,
                   *rest, attention_copies):
    if attention_copies:
        kvsb_bf_ref, kvmb_bf_ref, kmean_ref, hb_ref = rest
    else:
        (hb_ref,) = rest
    j = pl.program_id(1)

    @pl.when(j == 0)
    def _():
        hb_ref[...] = _layer_norm(x_ref[...], g_ref[...], b_ref[...]).astype(BF16)

    seg = j // QKV_STEPS_PER_SEG

    def project():
        return _dot(hb_ref[...], w_ref[...])

    def rope(a):
        cos = cos_ref[...]
        sin = sin_ref[...]
        heads = []
        for hh in range(a.shape[1] // HEAD_DIM):
            ah = a[:, hh * HEAD_DIM:(hh + 1) * HEAD_DIM]
            heads.append(ah * cos + pltpu.roll(ah, HEAD_DIM // 2, axis=1) * sin)
        return jnp.concatenate(heads, axis=1)

    @pl.when(seg == 0)
    def _():
        qsb_ref[...] = project().astype(BF16)

    @pl.when((seg == 1) | (seg == 2))
    def _():
        acc = project()
        kvsb_ref[...] = acc
        if attention_copies:
            kvsb_bf_ref[...] = acc.astype(BF16)

    @pl.when(seg == 3)
    def _():
        qmb_ref[...] = rope(project())

    @pl.when(seg == 4)
    def _():
        k = rope(project())
        kvmb_ref[...] = k
        if attention_copies:
            kvmb_bf_ref[...] = k.astype(BF16)
            for r in range(k.shape[0] // MOBA_BLOCK):
                kmean_ref[0, r:r + 1, :] = jnp.mean(k[r * MOBA_BLOCK:(r + 1) * MOBA_BLOCK, :], axis=0,
                                                    keepdims=True)

    @pl.when(seg == 5)
    def _():
        acc = project()
        kvmb_ref[...] = acc
        if attention_copies:
            kvmb_bf_ref[...] = acc.astype(BF16)


def _ln_qkv(x, g, b, cos, sin, w_bf16, tm, n_pos_tiles, attention_copies):
    rows, d = x.shape
    assert w_bf16.shape[1] == 3 * D_SB + 3 * D_MB and D_SB == D_MB
    sps = QKV_STEPS_PER_SEG
    grid = (rows // tm, 6 * sps)
    col = lambda j, first_seg, n_seg: jnp.minimum(jnp.maximum(j - first_seg * sps, 0), n_seg * sps - 1)
    out_specs = [
        pl.BlockSpec((tm, QKV_TN), lambda i, j: (i, col(j, 0, 1))),
        pl.BlockSpec((tm, QKV_TN), lambda i, j: (i, col(j, 1, 2))),
        pl.BlockSpec((tm, QKV_TN), lambda i, j: (i, col(j, 3, 1))),
        pl.BlockSpec((tm, QKV_TN), lambda i, j: (i, col(j, 4, 2))),
    ]
    out_shape = [
        jax.ShapeDtypeStruct((rows, D_SB), BF16),
        jax.ShapeDtypeStruct((rows, 2 * D_SB), F32),
        jax.ShapeDtypeStruct((rows, D_MB), F32),
        jax.ShapeDtypeStruct((rows, 2 * D_MB), F32),
    ]
    if attention_copies:
        assert tm % MOBA_BLOCK == 0
        bpt = tm // MOBA_BLOCK
        out_specs += [
            pl.BlockSpec((tm, QKV_TN), lambda i, j: (i, col(j, 1, 2))),
            pl.BlockSpec((tm, QKV_TN), lambda i, j: (i, col(j, 4, 2))),
            pl.BlockSpec((1, bpt, QKV_TN), lambda i, j: (i, 0, col(j, 4, 1))),
        ]
        out_shape += [
            jax.ShapeDtypeStruct((rows, 2 * D_SB), BF16),
            jax.ShapeDtypeStruct((rows, 2 * D_MB), BF16),
            jax.ShapeDtypeStruct((rows // tm, bpt, D_MB), F32),
        ]
    return pl.pallas_call(
        functools.partial(_ln_qkv_kernel, attention_copies=attention_copies),
        grid=grid,
        in_specs=[
            pl.BlockSpec((tm, d), lambda i, j: (i, 0)),
            pl.BlockSpec((1, d), lambda i, j: (0, 0)),
            pl.BlockSpec((1, d), lambda i, j: (0, 0)),
            pl.BlockSpec((tm, HEAD_DIM), lambda i, j: (i % n_pos_tiles, 0)),
            pl.BlockSpec((tm, HEAD_DIM), lambda i, j: (i % n_pos_tiles, 0)),
            pl.BlockSpec((d, QKV_TN), lambda i, j: (0, j)),
        ],
        out_specs=out_specs,
        out_shape=out_shape,
        scratch_shapes=[pltpu.VMEM((tm, d), BF16)],
        compiler_params=_cparams(("arbitrary", "arbitrary")),
        name="ln_qkv",
    )(x, g, b, cos, sin, w_bf16)


def _sb_prompt_kernel(q_ref, kv_ref, g_ref, o_ref, c_ref, acc_ref):
    qi = pl.program_id(1)
    nh, tq = N_HEADS_SB, SB_TQ
    tri = _tri(tq)
    row = lax.broadcasted_iota(jnp.int32, (nh * tq, tq), 0) % tq
    col = lax.broadcasted_iota(jnp.int32, (nh * tq, tq), 1)

    def head(h):
        return slice(h * HEAD_DIM, (h + 1) * HEAD_DIM)

    def attend(kb, valid):
        keys = pl.ds(pl.multiple_of(kb * tq, tq), tq)
        z = jnp.concatenate([_dot_nt(q_ref[:, head(h)], kv_ref[keys, head(h)]) for h in range(nh)],
                            axis=0) * SCALE
        sp = _softplus(z)
        l1mb = -sp if valid is None else jnp.where(valid, -sp, 0.0)
        hi, lo = _split_bf16(l1mb)
        c = c_ref[...]
        after = _dot(hi, tri) + _dot(lo, tri) + c
        w = jnp.exp(z - sp + after)
        if valid is not None:
            w = jnp.where(valid, w, 0.0)
        wb = w.astype(BF16)
        acc_ref[...] += jnp.concatenate(
            [_dot(wb[h * tq:(h + 1) * tq, :], kv_ref[keys, head(nh + h)]) for h in range(nh)], axis=0)
        c = c + jnp.sum(l1mb, axis=-1, keepdims=True)
        c_ref[...] = c
        return jnp.max(c)

    c_ref[...] = jnp.zeros(c_ref.shape, F32)
    acc_ref[...] = jnp.zeros(acc_ref.shape, F32)
    c_max = attend(qi, col < row)

    def cond(st):
        kb, c_max = st
        return (kb >= 0) & (c_max > SB_SKIP_BELOW)

    def body(st):
        kb, _ = st
        return kb - 1, attend(kb, None)

    lax.while_loop(cond, body, (qi - 1, c_max))
    for h in range(nh):
        o_ref[:, head(h)] = _head_rms(acc_ref[h * tq:(h + 1) * tq, :], g_ref[:, head(h)]).astype(BF16)


def _sb_prompt(q, kv_bf16, g, n_batch, t):
    rows = q.shape[0]
    nq = t // SB_TQ
    return pl.pallas_call(
        _sb_prompt_kernel,
        grid=(n_batch, nq),
        in_specs=[
            pl.BlockSpec((SB_TQ, D_SB), lambda b, i: (b * nq + i, 0)),
            pl.BlockSpec((t, 2 * D_SB), lambda b, i: (b, 0)),
            pl.BlockSpec((1, D_SB), lambda b, i: (0, 0)),
        ],
        out_specs=pl.BlockSpec((SB_TQ, D_SB), lambda b, i: (b * nq + i, 0)),
        out_shape=jax.ShapeDtypeStruct((rows, D_SB), BF16),
        scratch_shapes=[pltpu.VMEM((N_HEADS_SB * SB_TQ, 1), F32),
                        pltpu.VMEM((N_HEADS_SB * SB_TQ, HEAD_DIM), F32)],
        compiler_params=_cparams(("arbitrary", "arbitrary")),
        name="sb_prompt",
    )(q, kv_bf16, g)


def _topk_mask(gate, n_valid, axis=1):
    idx = lax.broadcasted_iota(jnp.int32, gate.shape, axis)
    idxf = idx.astype(F32)
    g = jnp.where(idx < n_valid, gate, NEG)
    sel = jnp.zeros(gate.shape, F32)
    for _ in range(MOBA_TOPK):
        m = jnp.max(g, axis=axis, keepdims=True)
        first = jnp.min(jnp.where(g == m, idxf, float(gate.shape[axis])), axis=axis, keepdims=True)
        pick = (idxf == first) & (m > 0.5 * NEG)
        sel = jnp.where(pick, 1.0, sel)
        g = jnp.where(pick, NEG, g)
    return sel


MOBA_CHUNK = 4
MOBA_HEADS_PER_STEP = 4
LOG2E = 1.4426950408889634


def _moba_prompt_kernel(q_ref, k_ref, v_ref, kmean_ref, g_ref, o_ref,
                        kaug_ref, qaug_ref, m_ref, l_ref, acc_ref, *, nb):
    qi = pl.program_id(2)
    tq = MOBA_BLOCK
    t = k_ref.shape[0]
    heads = range(MOBA_HEADS_PER_STEP)

    def head(h):
        return slice(h * HEAD_DIM, (h + 1) * HEAD_DIM)

    @pl.when(qi == 0)
    def _():
        blk = lax.broadcasted_iota(jnp.int32, (t, LANES), 0) // tq
        onehot = (blk == lax.broadcasted_iota(jnp.int32, (t, LANES), 1)).astype(BF16)
        for h in heads:
            kaug_ref[h, :, 0:HEAD_DIM] = k_ref[:, head(h)]
            kaug_ref[h, :, HEAD_DIM:HEAD_DIM + LANES] = onehot

    row = lax.broadcasted_iota(jnp.int32, (tq, tq), 0)
    col = lax.broadcasted_iota(jnp.int32, (tq, tq), 1)
    own = pl.ds(pl.multiple_of(qi * tq, tq), tq)

    for h in heads:
        qh = q_ref[:, head(h)].astype(BF16)
        gate_t = _dot_nt(kmean_ref[0, :, head(h)].astype(BF16), qh)
        sel_t = _topk_mask(gate_t, qi, axis=0)
        bias_t = jnp.concatenate([jnp.where(sel_t > 0.5, 0.0, NEG), jnp.full((LANES - nb, tq), NEG, F32)], axis=0)
        qaug_ref[h, :, 0:HEAD_DIM] = qh
        qaug_ref[h, :, HEAD_DIM:HEAD_DIM + LANES] = bias_t.T.astype(BF16)

        s = jnp.where(col <= row, _dot_nt(qh, k_ref[own, head(h)]) * (SCALE * LOG2E), NEG)
        m = jnp.max(s, axis=-1, keepdims=True)
        p = jnp.exp2(s - m)
        m_ref[h] = m
        l_ref[h] = jnp.sum(p, axis=-1, keepdims=True)
        acc_ref[h] = _dot(p.astype(BF16), v_ref[own, head(h)])

    def body(j, carry):
        keys = pl.ds(pl.multiple_of(j * (MOBA_CHUNK * tq), MOBA_CHUNK * tq), MOBA_CHUNK * tq)
        for h in heads:
            s = _dot_nt(qaug_ref[h], kaug_ref[h, keys, :]) * (SCALE * LOG2E)
            m = m_ref[h]
            m_new = jnp.maximum(m, jnp.max(s, axis=-1, keepdims=True))
            alpha = jnp.exp2(m - m_new)
            p = jnp.exp2(s - m_new)
            m_ref[h] = m_new
            l_ref[h] = alpha * l_ref[h] + jnp.sum(p, axis=-1, keepdims=True)
            acc_ref[h] = alpha * acc_ref[h] + _dot(p.astype(BF16), v_ref[keys, head(h)])
        return carry

    lax.fori_loop(0, (qi + MOBA_CHUNK - 1) // MOBA_CHUNK, body, 0)
    for h in heads:
        o_ref[:, head(h)] = _head_rms(acc_ref[h] / l_ref[h], g_ref[:, head(h)]).astype(BF16)


def _moba_prompt(q, kv_bf16, kmean, g, n_batch, t):
    rows = q.shape[0]
    tq = MOBA_BLOCK
    nb = t // tq
    hps = MOBA_HEADS_PER_STEP
    groups = N_HEADS_MB // hps
    width = hps * HEAD_DIM
    assert nb <= LANES and nb % MOBA_CHUNK == 0 and N_HEADS_MB % hps == 0
    return pl.pallas_call(
        functools.partial(_moba_prompt_kernel, nb=nb),
        grid=(n_batch, groups, nb),
        in_specs=[
            pl.BlockSpec((tq, width), lambda b, h, i: (b * nb + i, h)),
            pl.BlockSpec((t, width), lambda b, h, i: (b, h)),
            pl.BlockSpec((t, width), lambda b, h, i: (b, groups + h)),
            pl.BlockSpec((1, nb, width), lambda b, h, i: (b, 0, h)),
            pl.BlockSpec((1, width), lambda b, h, i: (0, h)),
        ],
        out_specs=pl.BlockSpec((tq, width), lambda b, h, i: (b * nb + i, h)),
        out_shape=jax.ShapeDtypeStruct((rows, D_MB), BF16),
        scratch_shapes=[pltpu.VMEM((hps, t, HEAD_DIM + LANES), BF16),
                        pltpu.VMEM((hps, tq, HEAD_DIM + LANES), BF16),
                        pltpu.VMEM((hps, tq, 1), F32),
                        pltpu.VMEM((hps, tq, 1), F32),
                        pltpu.VMEM((hps, tq, HEAD_DIM), F32)],
        compiler_params=_cparams(("arbitrary", "arbitrary", "arbitrary")),
        name="moba_prompt",
    )(q, kv_bf16, kv_bf16, kmean, g)


def _head_match(m_rows, n_cols, nh):
    r = lax.broadcasted_iota(jnp.int32, (m_rows, n_cols), 0)
    c = lax.broadcasted_iota(jnp.int32, (m_rows, n_cols), 1)
    return (r % nh) == (c % nh), r // nh, c // nh


def _sb_rows(q, k_rows, v_rows, c, acc, tri, valid):
    m_rows, n = q.shape[0], k_rows.shape[0]
    z = _dot_nt(q, k_rows) * SCALE
    sp = _softplus(z)
    l1mb = jnp.where(valid, -sp, 0.0)
    tiles = [l1mb[:, j * LANES:(j + 1) * LANES] for j in range(n // LANES)]
    later = [None] * len(tiles)
    for j in reversed(range(len(tiles))):
        later[j] = c
        c = c + jnp.sum(tiles[j], axis=-1, keepdims=True)
    hi, lo = _split_bf16(jnp.concatenate(tiles, axis=0))
    within = _dot(hi, tri) + _dot(lo, tri)
    after = jnp.concatenate(
        [within[j * m_rows:(j + 1) * m_rows] + later[j] for j in range(len(tiles))], axis=1)
    w = jnp.where(valid, jnp.exp(z - sp + after), 0.0)
    return c, acc + _dot(w.astype(BF16), v_rows)


def _sb_sample_kernel(pt_ref, q_ref, kn_ref, vn_ref, g_ref, cache_ref, o_ref,
                      buf_ref, newk_ref, newv_ref, sem, *, n_pages):
    b = pl.program_id(0)
    nh = N_HEADS_SB
    m_rows = q_ref.shape[1]
    n_new = kn_ref.shape[1]
    page_rows = PAGE_SIZE * nh

    def page_copy(p, slot, seq=b):
        return pltpu.make_async_copy(cache_ref.at[pt_ref[seq, p]], buf_ref.at[slot], sem.at[slot])

    @pl.when(b == 0)
    def _():
        page_copy(n_pages - 1, 0).start()

    newk_ref[...] = jnp.zeros(newk_ref.shape, F32)
    newv_ref[...] = jnp.zeros(newv_ref.shape, F32)
    newk_ref[0:n_new, :] = kn_ref[0]
    newv_ref[0:n_new, :] = vn_ref[0]

    q = q_ref[0]
    tri = _tri(LANES)

    match, qt, kt = _head_match(m_rows, LANES, nh)
    c, acc = _sb_rows(q, newk_ref[...].astype(BF16), newv_ref[...].astype(BF16),
                      jnp.zeros((m_rows, 1), F32), jnp.zeros((m_rows, HEAD_DIM), F32),
                      tri, match & (kt < qt))
    page_match, _, _ = _head_match(m_rows, page_rows, nh)

    def cond(st):
        p, c, _ = st
        return (p >= 0) & (jnp.max(c) > SB_SKIP_BELOW)

    def body(st):
        p, c, acc = st
        slot = (n_pages - 1 - p) % 2
        page_copy(p, slot).wait()

        @pl.when(p >= 1)
        def _():
            page_copy(p - 1, 1 - slot).start()

        k_rows = buf_ref[slot, :, 0].reshape(page_rows, HEAD_DIM).astype(BF16)
        v_rows = buf_ref[slot, :, 1].reshape(page_rows, HEAD_DIM).astype(BF16)
        c, acc = _sb_rows(q, k_rows, v_rows, c, acc, tri, page_match)
        return p - 1, c, acc

    p_end, _, acc = lax.while_loop(cond, body, (n_pages - 1, c, acc))

    @pl.when(p_end >= 0)
    def _():
        page_copy(p_end, (n_pages - 1 - p_end) % 2).wait()

    @pl.when(b + 1 < pl.num_programs(0))
    def _():
        page_copy(n_pages - 1, 0, seq=b + 1).start()

    o_ref[0] = _head_rms(acc, g_ref[...]).astype(BF16)


def _sb_sample(page_table, q, k_new, v_new, g_rows, cache):
    nb, n_pages = page_table.shape
    m_rows, n_new = q.shape[1], k_new.shape[1]
    nh = cache.shape[3]
    assert n_new <= LANES and m_rows % 8 == 0
    grid_spec = pltpu.PrefetchScalarGridSpec(
        num_scalar_prefetch=1,
        grid=(nb,),
        in_specs=[
            pl.BlockSpec((1, m_rows, HEAD_DIM), lambda b, pt: (b, 0, 0)),
            pl.BlockSpec((1, n_new, HEAD_DIM), lambda b, pt: (b, 0, 0)),
            pl.BlockSpec((1, n_new, HEAD_DIM), lambda b, pt: (b, 0, 0)),
            pl.BlockSpec((m_rows, HEAD_DIM), lambda b, pt: (0, 0)),
            pl.BlockSpec(memory_space=pl.ANY),
        ],
        out_specs=pl.BlockSpec((1, m_rows, HEAD_DIM), lambda b, pt: (b, 0, 0)),
        scratch_shapes=[pltpu.VMEM((2, PAGE_SIZE, 2, nh, HEAD_DIM), F32),
                        pltpu.VMEM((LANES, HEAD_DIM), F32),
                        pltpu.VMEM((LANES, HEAD_DIM), F32),
                        pltpu.SemaphoreType.DMA((2,))],
    )
    return pl.pallas_call(
        functools.partial(_sb_sample_kernel, n_pages=n_pages),
        grid_spec=grid_spec,
        out_shape=jax.ShapeDtypeStruct((nb, m_rows, HEAD_DIM), BF16),
        compiler_params=_cparams(("arbitrary",)),
        name="sb_sample",
    )(page_table, q, k_new, v_new, g_rows, cache)


MOBA_PAGES_PER_STEP = 16


def _moba_sample_kernel(pt_ref, q_ref, kn_ref, vn_ref, g_ref, *rest, n_pages):
    page_refs = rest[:MOBA_PAGES_PER_STEP]
    o_ref, newk_ref, newv_ref, m_ref, l_ref, s_ref, oall_ref = rest[MOBA_PAGES_PER_STEP:]
    step = pl.program_id(1)
    nh = N_HEADS_MB
    m_rows = q_ref.shape[1]
    n_new = kn_ref.shape[1]
    page_rows = PAGE_SIZE * nh
    ppb = MOBA_BLOCK // PAGE_SIZE
    bps = MOBA_PAGES_PER_STEP // ppb

    qh, ql = _split_bf16(q_ref[0])
    qq = jnp.concatenate([qh, ql], axis=0)

    def partial_softmax(k_rows, v_rows, valid):
        qk2 = _dot_nt(qq, k_rows)
        qk = qk2[0:m_rows] + qk2[m_rows:2 * m_rows]
        s = jnp.where(valid, qk * SCALE, NEG)
        m = jnp.max(s, axis=-1, keepdims=True)
        e = jnp.exp(s - m)
        l = jnp.sum(e, axis=-1, keepdims=True)
        o = _dot(e.astype(BF16), v_rows)
        return jnp.sum(jnp.where(valid, qk, 0.0), axis=-1, keepdims=True), m, l, o

    lane = lax.broadcasted_iota(jnp.int32, (m_rows, LANES), 1)

    @pl.when(step == 0)
    def _():
        m_ref[...] = jnp.full(m_ref.shape, NEG, F32)
        l_ref[...] = jnp.zeros(l_ref.shape, F32)
        s_ref[...] = jnp.zeros(s_ref.shape, F32)

    block_match, _, _ = _head_match(m_rows, ppb * page_rows, nh)

    def block_rows(i, kv):
        return jnp.concatenate([r[0, :, kv].reshape(page_rows, HEAD_DIM).astype(BF16)
                                for r in page_refs[i * ppb:(i + 1) * ppb]], axis=0)

    qk2 = [_dot_nt(qq, block_rows(i, 0)) for i in range(bps)]
    es = []
    m_all, l_all, s_all = m_ref[...], l_ref[...], s_ref[...]
    for i in range(bps):
        qk = qk2[i][0:m_rows] + qk2[i][m_rows:2 * m_rows]
        s = jnp.where(block_match, qk * SCALE, NEG)
        m = jnp.max(s, axis=-1, keepdims=True)
        e = jnp.exp(s - m)
        es.append(e.astype(BF16))
        here = lane == step * bps + i
        m_all = jnp.where(here, m, m_all)
        l_all = jnp.where(here, jnp.sum(e, axis=-1, keepdims=True), l_all)
        s_all = jnp.where(here, jnp.sum(jnp.where(block_match, qk, 0.0), axis=-1, keepdims=True), s_all)
    m_ref[...], l_ref[...], s_ref[...] = m_all, l_all, s_all
    for i in range(bps):
        oall_ref[step * bps + i] = _dot(es[i], block_rows(i, 1))

    @pl.when(step == pl.num_programs(1) - 1)
    def _():
        n_blocks = n_pages // ppb
        sel = _topk_mask(s_ref[...] * (1.0 / MOBA_BLOCK), n_blocks)

        newk_ref[...] = jnp.zeros(newk_ref.shape, F32)
        newv_ref[...] = jnp.zeros(newv_ref.shape, F32)
        newk_ref[0:n_new, :] = kn_ref[0]
        newv_ref[0:n_new, :] = vn_ref[0]
        match, qt, kt = _head_match(m_rows, LANES, nh)
        _, m_own, l_own, o_own = partial_softmax(newk_ref[...].astype(BF16), newv_ref[...].astype(BF16),
                                                 match & (kt <= qt))

        m_all = m_ref[...]
        m_tot = jnp.maximum(jnp.max(jnp.where(sel > 0.5, m_all, NEG), axis=-1, keepdims=True), m_own)
        wgt = jnp.where(sel > 0.5, jnp.exp(m_all - m_tot), 0.0)
        w_own = jnp.exp(m_own - m_tot)
        l_tot = jnp.sum(wgt * l_ref[...], axis=-1, keepdims=True) + w_own * l_own

        def merge(n, acc):
            w = jnp.sum(jnp.where(lane == n, wgt, 0.0), axis=-1, keepdims=True)
            return acc + w * oall_ref[n]

        acc = lax.fori_loop(0, n_blocks, merge, w_own * o_own)
        o_ref[0] = _head_rms(acc / l_tot, g_ref[...]).astype(BF16)


def _moba_sample(page_table, q, k_new, v_new, g_rows, cache):
    nb, n_pages = page_table.shape
    m_rows, n_new = q.shape[1], k_new.shape[1]
    nh = cache.shape[3]
    pps = MOBA_PAGES_PER_STEP
    assert n_pages <= LANES and MOBA_BLOCK == 2 * PAGE_SIZE and n_new <= LANES
    assert pps % 2 == 0 and n_pages % pps == 0
    page_spec = lambda i: pl.BlockSpec((1, PAGE_SIZE, 2, nh, HEAD_DIM),
                                       lambda b, s, pt: (pt[b, s * pps + i], 0, 0, 0, 0))
    grid_spec = pltpu.PrefetchScalarGridSpec(
        num_scalar_prefetch=1,
        grid=(nb, n_pages // pps),
        in_specs=[
            pl.BlockSpec((1, m_rows, HEAD_DIM), lambda b, s, pt: (b, 0, 0)),
            pl.BlockSpec((1, n_new, HEAD_DIM), lambda b, s, pt: (b, 0, 0)),
            pl.BlockSpec((1, n_new, HEAD_DIM), lambda b, s, pt: (b, 0, 0)),
            pl.BlockSpec((m_rows, HEAD_DIM), lambda b, s, pt: (0, 0)),
        ] + [page_spec(i) for i in range(pps)],
        out_specs=pl.BlockSpec((1, m_rows, HEAD_DIM), lambda b, s, pt: (b, 0, 0)),
        scratch_shapes=[pltpu.VMEM((LANES, HEAD_DIM), F32),
                        pltpu.VMEM((LANES, HEAD_DIM), F32),
                        pltpu.VMEM((m_rows, LANES), F32),
                        pltpu.VMEM((m_rows, LANES), F32),
                        pltpu.VMEM((m_rows, LANES), F32),
                        pltpu.VMEM((n_pages * PAGE_SIZE // MOBA_BLOCK, m_rows, HEAD_DIM), F32)],
    )
    return pl.pallas_call(
        functools.partial(_moba_sample_kernel, n_pages=n_pages),
        grid_spec=grid_spec,
        out_shape=jax.ShapeDtypeStruct((nb, m_rows, HEAD_DIM), BF16),
        compiler_params=_cparams(("arbitrary", "arbitrary")),
        name="moba_sample",
    )(page_table, q, k_new, v_new, g_rows, *([cache] * pps))


def _out_proj_kernel(osb_ref, omb_ref, x_ref, g0_ref, b0_ref, wa_ref, wb_ref, g_ref, b_ref, h1_ref, h1b_ref,
                     *, alpha):
    mix = _dot(osb_ref[...], wa_ref[...]) + _dot(omb_ref[...], wb_ref[...])
    h = _layer_norm(x_ref[...], g0_ref[...], b0_ref[...])
    h1 = _layer_norm(alpha * h + mix, g_ref[...], b_ref[...])
    h1_ref[...] = h1
    h1b_ref[...] = h1.astype(BF16)


def _out_proj(o_sb, o_mb, x, ln_in_g, ln_in_b, w_out_bf16, g, b, tm, alpha):
    rows, d = x.shape
    row_spec = pl.BlockSpec((1, d), lambda i: (0, 0))
    return pl.pallas_call(
        functools.partial(_out_proj_kernel, alpha=alpha),
        grid=(rows // tm,),
        in_specs=[
            pl.BlockSpec((tm, D_SB), lambda i: (i, 0)),
            pl.BlockSpec((tm, D_MB), lambda i: (i, 0)),
            pl.BlockSpec((tm, d), lambda i: (i, 0)),
            row_spec,
            row_spec,
            pl.BlockSpec((D_SB, d), lambda i: (0, 0)),
            pl.BlockSpec((D_MB, d), lambda i: (1, 0)),
            row_spec,
            row_spec,
        ],
        out_specs=[pl.BlockSpec((tm, d), lambda i: (i, 0)), pl.BlockSpec((tm, d), lambda i: (i, 0))],
        out_shape=[jax.ShapeDtypeStruct((rows, d), F32), jax.ShapeDtypeStruct((rows, d), BF16)],
        compiler_params=_cparams(("arbitrary",)),
        name="out_proj",
    )(o_sb, o_mb, x, ln_in_g, ln_in_b, w_out_bf16, w_out_bf16, g, b)


HALO = 16


def _conv_ffn_kernel(*refs, alpha, seq_len, tail, has_state):
    if has_state:
        (h1_ref, h1b_ref, halo_ref, wg_ref, wu_ref, wd_ref, cw_ref, cb_ref, g_ref, b_ref, p1_ref, p2_ref,
         out_ref, gt_ref, x_ref, gbuf_ref, acc_ref) = refs
    else:
        (h1_ref, h1b_ref, halo_ref, wg_ref, wu_ref, wd_ref, cw_ref, cb_ref, g_ref, b_ref,
         out_ref, gt_ref, x_ref, gbuf_ref, acc_ref) = refs
    i = pl.program_id(0)
    j = pl.program_id(1)
    tm = h1_ref.shape[0]

    @pl.when(j == 0)
    def _():
        x_ref[0:HALO, :] = halo_ref[...]
        x_ref[HALO:HALO + tm, :] = h1b_ref[...]
        acc_ref[...] = jnp.zeros(acc_ref.shape, F32)

    gbuf_ref[...] = _dot(x_ref[...], wg_ref[...])
    u = _dot(x_ref[HALO:HALO + tm, :], wu_ref[...])
    g0 = gbuf_ref[HALO:HALO + tm, :]
    g1 = gbuf_ref[HALO - 1:HALO - 1 + tm, :]
    g2 = gbuf_ref[HALO - 2:HALO - 2 + tm, :]
    pos = (i * tm + lax.broadcasted_iota(jnp.int32, (tm, 1), 0)) % seq_len
    g1 = jnp.where(pos >= 1, g1, p1_ref[...] if has_state else 0.0)
    g2 = jnp.where(pos >= 2, g2, p2_ref[...] if has_state else 0.0)
    cw = cw_ref[...]
    c = cb_ref[...] + cw[0:1, :] * g2 + cw[1:2, :] * g1 + cw[2:3, :] * g0
    a = 0.5 * c * (1.0 + lax.erf(c * math.sqrt(0.5))) * u
    acc_ref[...] += _dot(a.astype(BF16), wd_ref[...])
    gt_ref[0] = g0[tm - tail:, :]

    @pl.when(j == pl.num_programs(1) - 1)
    def _():
        out_ref[...] = _layer_norm(alpha * h1_ref[...] + acc_ref[...], g_ref[...], b_ref[...])


def _conv_ffn(h1, h1b, wg, wu, wd, conv_w, conv_b, g, b, tm, tf, seq_len, tail, alpha, state=None):
    rows, d = h1.shape
    f = wg.shape[1]
    has_state = state is not None
    in_specs = [
        pl.BlockSpec((tm, d), lambda i, j: (i, 0)),
        pl.BlockSpec((tm, d), lambda i, j: (i, 0)),
        pl.BlockSpec((HALO, d), lambda i, j: (jnp.maximum(i * (tm // HALO) - 1, 0), 0)),
        pl.BlockSpec((d, tf), lambda i, j: (0, j)),
        pl.BlockSpec((d, tf), lambda i, j: (0, j)),
        pl.BlockSpec((tf, d), lambda i, j: (j, 0)),
        pl.BlockSpec((CONV_WIDTH, tf), lambda i, j: (0, j)),
        pl.BlockSpec((1, tf), lambda i, j: (0, j)),
        pl.BlockSpec((1, d), lambda i, j: (0, 0)),
        pl.BlockSpec((1, d), lambda i, j: (0, 0)),
    ]
    args = [h1, h1b, h1b, wg, wu, wd, conv_w, conv_b, g, b]
    if has_state:
        in_specs += [pl.BlockSpec((tm, tf), lambda i, j: (i, j)), pl.BlockSpec((tm, tf), lambda i, j: (i, j))]
        args += list(state)
    return pl.pallas_call(
        functools.partial(_conv_ffn_kernel, alpha=alpha, seq_len=seq_len, tail=tail, has_state=has_state),
        grid=(rows // tm, f // tf),
        in_specs=in_specs,
        out_specs=[pl.BlockSpec((tm, d), lambda i, j: (i, 0)),
                   pl.BlockSpec((1, tail, tf), lambda i, j: (i, 0, j))],
        out_shape=[jax.ShapeDtypeStruct((rows, d), F32),
                   jax.ShapeDtypeStruct((rows // tm, tail, f), F32)],
        scratch_shapes=[pltpu.VMEM((HALO + tm, d), BF16), pltpu.VMEM((HALO + tm, tf), F32),
                        pltpu.VMEM((tm, d), F32)],
        compiler_params=_cparams(("arbitrary", "arbitrary")),
        name="conv_ffn",
    )(*args)


def _rope_tables(pos):
    half = HEAD_DIM // 2
    inv = ROPE_THETA ** (-jnp.arange(half, dtype=F32) / half)
    ang = pos.astype(F32)[:, None] * inv[None, :]
    cos, sin = jnp.cos(ang), jnp.sin(ang)
    return jnp.concatenate([cos, cos], axis=-1), jnp.concatenate([-sin, sin], axis=-1)


def _row(v):
    return v.reshape(1, -1)


def kernel(x_prompt, x_sample, cache_kv_sb, cache_kv_mb, state_conv, page_table, ln_in_g, ln_in_b, w_in, g_sb, g_mb, w_out, ln1_g, ln1_b, w_gate, w_up, conv_w, conv_b, w_down, ln2_g, ln2_b):
    depth = w_in.shape[0]
    n_p, t_p, d = x_prompt.shape
    n_s, t_s, _ = x_sample.shape
    f = w_gate.shape[-1]
    past = page_table.shape[1] * PAGE_SIZE
    alpha = (2 * depth) ** 0.25
    assert depth == 1, "the input LayerNorm is fused into the first layer's projection"
    assert t_p % MOBA_BLOCK == 0 and past % MOBA_BLOCK == 0
    l = 0

    tm_p, tf = 512, 512
    tm_qkv = 1024
    rows_p, rows_s = n_p * t_p, n_s * t_s
    xp, xs = x_prompt.reshape(rows_p, d), x_sample.reshape(rows_s, d)
    cos_p, sin_p = _rope_tables(jnp.arange(t_p, dtype=jnp.int32))
    cos_s, sin_s = _rope_tables(past + jnp.arange(t_s, dtype=jnp.int32))
    cos_s, sin_s = jnp.tile(cos_s, (n_s, 1)), jnp.tile(sin_s, (n_s, 1))

    ln_g, ln_b = _row(ln_in_g), _row(ln_in_b)
    w_in_b = w_in[l].astype(BF16)
    w_out_b = w_out[l].astype(BF16)
    wg_b, wu_b, wd_b = w_gate[l].astype(BF16), w_up[l].astype(BF16), w_down[l].astype(BF16)
    ffn_consts = (conv_w[l], _row(conv_b[l]), _row(ln2_g[l]), _row(ln2_b[l]))

    qsb, kvsb, qmb, kvmb, kvsb_bf, kvmb_bf, kmean = _ln_qkv(
        xp, ln_g, ln_b, cos_p, sin_p, w_in_b, tm_qkv, t_p // tm_qkv, True)
    o_sb = _sb_prompt(qsb, kvsb_bf, _row(g_sb[l]), n_p, t_p)
    o_mb = _moba_prompt(qmb, kvmb_bf, kmean.reshape(n_p, t_p // MOBA_BLOCK, D_MB), _row(g_mb[l]), n_p, t_p)
    h1, h1b = _out_proj(o_sb, o_mb, xp, ln_g, ln_b, w_out_b, _row(ln1_g[l]), _row(ln1_b[l]), tm_p, alpha)
    y_p, g_tail = _conv_ffn(h1, h1b, wg_b, wu_b, wd_b, *ffn_consts, tm_p, tf, t_p, 8, alpha)
    kv_sb_p = kvsb.reshape(n_p, t_p, 2, N_HEADS_SB, HEAD_DIM)
    kv_mb_p = kvmb.reshape(n_p, t_p, 2, N_HEADS_MB, HEAD_DIM)
    tiles_per_seq = t_p // tm_p
    conv_p = g_tail[tiles_per_seq - 1::tiles_per_seq, 8 - (CONV_WIDTH - 1):, :]

    qsb, kvsb, qmb, kvmb = _ln_qkv(xs, ln_g, ln_b, cos_s, sin_s, w_in_b, rows_s, 1, False)
    kv_sb_s = kvsb.reshape(n_s, t_s, 2, N_HEADS_SB, HEAD_DIM)
    kv_mb_s = kvmb.reshape(n_s, t_s, 2, N_HEADS_MB, HEAD_DIM)
    o_sb = _sb_sample(page_table, qsb.reshape(n_s, t_s * N_HEADS_SB, HEAD_DIM),
                      kv_sb_s[:, :, 0].reshape(n_s, t_s * N_HEADS_SB, HEAD_DIM),
                      kv_sb_s[:, :, 1].reshape(n_s, t_s * N_HEADS_SB, HEAD_DIM),
                      jnp.tile(g_sb[l].reshape(N_HEADS_SB, HEAD_DIM), (t_s, 1)), cache_kv_sb[l])
    o_mb = _moba_sample(page_table, qmb.reshape(n_s, t_s * N_HEADS_MB, HEAD_DIM),
                        kv_mb_s[:, :, 0].reshape(n_s, t_s * N_HEADS_MB, HEAD_DIM),
                        kv_mb_s[:, :, 1].reshape(n_s, t_s * N_HEADS_MB, HEAD_DIM),
                        jnp.tile(g_mb[l].reshape(N_HEADS_MB, HEAD_DIM), (t_s, 1)), cache_kv_mb[l])
    h1, h1b = _out_proj(o_sb.reshape(rows_s, D_SB), o_mb.reshape(rows_s, D_MB), xs, ln_g, ln_b, w_out_b,
                        _row(ln1_g[l]), _row(ln1_b[l]), rows_s, alpha)
    sc = state_conv[l].astype(F32)
    zero = jnp.zeros((n_s, t_s, f), F32)
    prev1 = zero.at[:, 0].set(sc[:, 1]).reshape(rows_s, f)
    prev2 = zero.at[:, 0].set(sc[:, 0]).at[:, 1].set(sc[:, 1]).reshape(rows_s, f)
    y_s, g_all = _conv_ffn(h1, h1b, wg_b, wu_b, wd_b, *ffn_consts, rows_s, tf, t_s, rows_s, alpha,
                           state=(prev1, prev2))
    conv_s = g_all.reshape(n_s, t_s, f)[:, t_s - (CONV_WIDTH - 1):]

    return (y_p.reshape(n_p, t_p, d), y_s.reshape(n_s, t_s, d), kv_sb_p[None], kv_mb_p[None], conv_p[None],
            kv_sb_s[None], kv_mb_s[None], conv_s[None])
```

```python
import functools
import math

import jax
import jax.numpy as jnp
from jax import lax
from jax.experimental import pallas as pl
from jax.experimental.pallas import tpu as pltpu

F32 = jnp.float32
BF16 = jnp.bfloat16

HEAD_DIM = 128
N_HEADS_SB = 8
N_HEADS_MB = 8
D_SB = N_HEADS_SB * HEAD_DIM
D_MB = N_HEADS_MB * HEAD_DIM
MOBA_BLOCK = 256
MOBA_TOPK = 3
ROPE_THETA = 10000.0
CONV_WIDTH = 3
PAGE_SIZE = 128
LN_EPS = 1e-5
RMS_EPS = 1e-6
NEG = -1e30
SCALE = 1.0 / math.sqrt(HEAD_DIM)

SB_SKIP_BELOW = -105.0

SB_TQ = 128
LANES = 128
VMEM_LIMIT_V7X = 56 * 1024 * 1024


def _cparams(sem):
    return pltpu.CompilerParams(dimension_semantics=sem, vmem_limit_bytes=VMEM_LIMIT_V7X)


def _layer_norm(x, g, b):
    mu = jnp.mean(x, axis=-1, keepdims=True)
    xc = x - mu
    var = jnp.mean(xc * xc, axis=-1, keepdims=True)
    return xc * lax.rsqrt(var + LN_EPS) * g + b


def _head_rms(o, g):
    ms = jnp.mean(o * o, axis=-1, keepdims=True)
    return o * lax.rsqrt(ms + RMS_EPS) * g


def _dot_nt(a, b):
    return lax.dot_general(a, b, (((1,), (1,)), ((), ())), preferred_element_type=F32)


def _dot(a, b):
    return jnp.dot(a, b, preferred_element_type=F32)


def _split_bf16(x):
    hi = x.astype(BF16)
    lo = (x - hi.astype(F32)).astype(BF16)
    return hi, lo


def _softplus(z):
    return jnp.maximum(z, 0.0) + jnp.log(1.0 + jnp.exp(-jnp.abs(z)))


def _tri(n):
    r = lax.broadcasted_iota(jnp.int32, (n, n), 0)
    s = lax.broadcasted_iota(jnp.int32, (n, n), 1)
    return (r > s).astype(BF16)


QKV_TN = 512
QKV_STEPS_PER_SEG = D_SB // QKV_TN

def _ln_qkv_kernel(x_ref, g_ref, b_ref, cos_ref, sin_ref, w_ref, qsb_ref, kvsb_ref, qmb_ref, kvmb_ref,
                   *rest, attention_copies):
    if attention_copies:
        kvsb_bf_ref, kvmb_bf_ref, kmean_ref, hb_ref = rest
    else:
        (hb_ref,) = rest
    j = pl.program_id(1)

    @pl.when(j == 0)
    def _():
        hb_ref[...] = _layer_norm(x_ref[...], g_ref[...], b_ref[...]).astype(BF16)

    seg = j // QKV_STEPS_PER_SEG

    def project():
        return _dot(hb_ref[...], w_ref[...])

    def rope(a):
        cos = cos_ref[...]
        sin = sin_ref[...]
        heads = []
        for hh in range(a.shape[1] // HEAD_DIM):
            ah = a[:, hh * HEAD_DIM:(hh + 1) * HEAD_DIM]
            heads.append(ah * cos + pltpu.roll(ah, HEAD_DIM // 2, axis=1) * sin)
        return jnp.concatenate(heads, axis=1)

    @pl.when(seg == 0)
    def _():
        qsb_ref[...] = project().astype(BF16)

    @pl.when((seg == 1) | (seg == 2))
    def _():
        acc = project()
        kvsb_ref[...] = acc
        if attention_copies:
            kvsb_bf_ref[...] = acc.astype(BF16)

    @pl.when(seg == 3)
    def _():
        qmb_ref[...] = rope(project())

    @pl.when(seg == 4)
    def _():
        k = rope(project())
        kvmb_ref[...] = k
        if attention_copies:
            kvmb_bf_ref[...] = k.astype(BF16)
            for r in range(k.shape[0] // MOBA_BLOCK):
                kmean_ref[0, r:r + 1, :] = jnp.mean(k[r * MOBA_BLOCK:(r + 1) * MOBA_BLOCK, :], axis=0,
                                                    keepdims=True)

    @pl.when(seg == 5)
    def _():
        acc = project()
        kvmb_ref[...] = acc
        if attention_copies:
            kvmb_bf_ref[...] = acc.astype(BF16)


def _ln_qkv(x, g, b, cos, sin, w_bf16, tm, n_pos_tiles, attention_copies):
    rows, d = x.shape
    assert w_bf16.shape[1] == 3 * D_SB + 3 * D_MB and D_SB == D_MB
    sps = QKV_STEPS_PER_SEG
    grid = (rows // tm, 6 * sps)
    col = lambda j, first_seg, n_seg: jnp.minimum(jnp.maximum(j - first_seg * sps, 0), n_seg * sps - 1)
    out_specs = [
        pl.BlockSpec((tm, QKV_TN), lambda i, j: (i, col(j, 0, 1))),
        pl.BlockSpec((tm, QKV_TN), lambda i, j: (i, col(j, 1, 2))),
        pl.BlockSpec((tm, QKV_TN), lambda i, j: (i, col(j, 3, 1))),
        pl.BlockSpec((tm, QKV_TN), lambda i, j: (i, col(j, 4, 2))),
    ]
    out_shape = [
        jax.ShapeDtypeStruct((rows, D_SB), BF16),
        jax.ShapeDtypeStruct((rows, 2 * D_SB), F32),
        jax.ShapeDtypeStruct((rows, D_MB), F32),
        jax.ShapeDtypeStruct((rows, 2 * D_MB), F32),
    ]
    if attention_copies:
        assert tm % MOBA_BLOCK == 0
        bpt = tm // MOBA_BLOCK
        out_specs += [
            pl.BlockSpec((tm, QKV_TN), lambda i, j: (i, col(j, 1, 2))),
            pl.BlockSpec((tm, QKV_TN), lambda i, j: (i, col(j, 4, 2))),
            pl.BlockSpec((1, bpt, QKV_TN), lambda i, j: (i, 0, col(j, 4, 1))),
        ]
        out_shape += [
            jax.ShapeDtypeStruct((rows, 2 * D_SB), BF16),
            jax.ShapeDtypeStruct((rows, 2 * D_MB), BF16),
            jax.ShapeDtypeStruct((rows // tm, bpt, D_MB), F32),
        ]
    return pl.pallas_call(
        functools.partial(_ln_qkv_kernel, attention_copies=attention_copies),
        grid=grid,
        in_specs=[
            pl.BlockSpec((tm, d), lambda i, j: (i, 0)),
            pl.BlockSpec((1, d), lambda i, j: (0, 0)),
            pl.BlockSpec((1, d), lambda i, j: (0, 0)),
            pl.BlockSpec((tm, HEAD_DIM), lambda i, j: (i % n_pos_tiles, 0)),
            pl.BlockSpec((tm, HEAD_DIM), lambda i, j: (i % n_pos_tiles, 0)),
            pl.BlockSpec((d, QKV_TN), lambda i, j: (0, j)),
        ],
        out_specs=out_specs,
        out_shape=out_shape,
        scratch_shapes=[pltpu.VMEM((tm, d), BF16)],
        compiler_params=_cparams(("arbitrary", "arbitrary")),
        name="ln_qkv",
    )(x, g, b, cos, sin, w_bf16)


def _sb_prompt_kernel(q_ref, kv_ref, g_ref, o_ref, c_ref, acc_ref):
    qi = pl.program_id(1)
    nh, tq = N_HEADS_SB, SB_TQ
    tri = _tri(tq)
    row = lax.broadcasted_iota(jnp.int32, (nh * tq, tq), 0) % tq
    col = lax.broadcasted_iota(jnp.int32, (nh * tq, tq), 1)

    def head(h):
        return slice(h * HEAD_DIM, (h + 1) * HEAD_DIM)

    def attend(kb, valid):
        keys = pl.ds(pl.multiple_of(kb * tq, tq), tq)
        z = jnp.concatenate([_dot_nt(q_ref[:, head(h)], kv_ref[keys, head(h)]) for h in range(nh)],
                            axis=0) * SCALE
        sp = _softplus(z)
        l1mb = -sp if valid is None else jnp.where(valid, -sp, 0.0)
        hi, lo = _split_bf16(l1mb)
        c = c_ref[...]
        after = _dot(hi, tri) + _dot(lo, tri) + c
        w = jnp.exp(z - sp + after)
        if valid is not None:
            w = jnp.where(valid, w, 0.0)
        wb = w.astype(BF16)
        acc_ref[...] += jnp.concatenate(
            [_dot(wb[h * tq:(h + 1) * tq, :], kv_ref[keys, head(nh + h)]) for h in range(nh)], axis=0)
        c = c + jnp.sum(l1mb, axis=-1, keepdims=True)
        c_ref[...] = c
        return jnp.max(c)

    c_ref[...] = jnp.zeros(c_ref.shape, F32)
    acc_ref[...] = jnp.zeros(acc_ref.shape, F32)
    c_max = attend(qi, col < row)

    def cond(st):
        kb, c_max = st
        return (kb >= 0) & (c_max > SB_SKIP_BELOW)

    def body(st):
        kb, _ = st
        return kb - 1, attend(kb, None)

    lax.while_loop(cond, body, (qi - 1, c_max))
    for h in range(nh):
        o_ref[:, head(h)] = _head_rms(acc_ref[h * tq:(h + 1) * tq, :], g_ref[:, head(h)]).astype(BF16)


def _sb_prompt(q, kv_bf16, g, n_batch, t):
    rows = q.shape[0]
    nq = t // SB_TQ
    return pl.pallas_call(
        _sb_prompt_kernel,
        grid=(n_batch, nq),
        in_specs=[
            pl.BlockSpec((SB_TQ, D_SB), lambda b, i: (b * nq + i, 0)),
            pl.BlockSpec((t, 2 * D_SB), lambda b, i: (b, 0)),
            pl.BlockSpec((1, D_SB), lambda b, i: (0, 0)),
        ],
        out_specs=pl.BlockSpec((SB_TQ, D_SB), lambda b, i: (b * nq + i, 0)),
        out_shape=jax.ShapeDtypeStruct((rows, D_SB), BF16),
        scratch_shapes=[pltpu.VMEM((N_HEADS_SB * SB_TQ, 1), F32),
                        pltpu.VMEM((N_HEADS_SB * SB_TQ, HEAD_DIM), F32)],
        compiler_params=_cparams(("arbitrary", "arbitrary")),
        name="sb_prompt",
    )(q, kv_bf16, g)


def _topk_mask(gate, n_valid, axis=1):
    idx = lax.broadcasted_iota(jnp.int32, gate.shape, axis)
    idxf = idx.astype(F32)
    g = jnp.where(idx < n_valid, gate, NEG)
    sel = jnp.zeros(gate.shape, F32)
    for _ in range(MOBA_TOPK):
        m = jnp.max(g, axis=axis, keepdims=True)
        first = jnp.min(jnp.where(g == m, idxf, float(gate.shape[axis])), axis=axis, keepdims=True)
        pick = (idxf == first) & (m > 0.5 * NEG)
        sel = jnp.where(pick, 1.0, sel)
        g = jnp.where(pick, NEG, g)
    return sel


MOBA_CHUNK = 4
MOBA_HEADS_PER_STEP = 4
LOG2E = 1.4426950408889634


def _moba_prompt_kernel(q_ref, k_ref, v_ref, kmean_ref, g_ref, o_ref,
                        kaug_ref, qaug_ref, m_ref, l_ref, acc_ref, *, nb):
    qi = pl.program_id(2)
    tq = MOBA_BLOCK
    t = k_ref.shape[0]
    heads = range(MOBA_HEADS_PER_STEP)

    def head(h):
        return slice(h * HEAD_DIM, (h + 1) * HEAD_DIM)

    @pl.when(qi == 0)
    def _():
        blk = lax.broadcasted_iota(jnp.int32, (t, LANES), 0) // tq
        onehot = (blk == lax.broadcasted_iota(jnp.int32, (t, LANES), 1)).astype(BF16)
        for h in heads:
            kaug_ref[h, :, 0:HEAD_DIM] = k_ref[:, head(h)]
            kaug_ref[h, :, HEAD_DIM:HEAD_DIM + LANES] = onehot

    row = lax.broadcasted_iota(jnp.int32, (tq, tq), 0)
    col = lax.broadcasted_iota(jnp.int32, (tq, tq), 1)
    own = pl.ds(pl.multiple_of(qi * tq, tq), tq)

    for h in heads:
        qh = q_ref[:, head(h)].astype(BF16)
        gate_t = _dot_nt(kmean_ref[0, :, head(h)].astype(BF16), qh)
        sel_t = _topk_mask(gate_t, qi, axis=0)
        bias_t = jnp.concatenate([jnp.where(sel_t > 0.5, 0.0, NEG), jnp.full((LANES - nb, tq), NEG, F32)], axis=0)
        qaug_ref[h, :, 0:HEAD_DIM] = qh
        qaug_ref[h, :, HEAD_DIM:HEAD_DIM + LANES] = bias_t.T.astype(BF16)

        s = jnp.where(col <= row, _dot_nt(qh, k_ref[own, head(h)]) * (SCALE * LOG2E), NEG)
        m = jnp.max(s, axis=-1, keepdims=True)
        p = jnp.exp2(s - m)
        m_ref[h] = m
        l_ref[h] = jnp.sum(p, axis=-1, keepdims=True)
        acc_ref[h] = _dot(p.astype(BF16), v_ref[own, head(h)])

    def body(j, carry):
        keys = pl.ds(pl.multiple_of(j * (MOBA_CHUNK * tq), MOBA_CHUNK * tq), MOBA_CHUNK * tq)
        for h in heads:
            s = _dot_nt(qaug_ref[h], kaug_ref[h, keys, :]) * (SCALE * LOG2E)
            m = m_ref[h]
            m_new = jnp.maximum(m, jnp.max(s, axis=-1, keepdims=True))
            alpha = jnp.exp2(m - m_new)
            p = jnp.exp2(s - m_new)
            m_ref[h] = m_new
            l_ref[h] = alpha * l_ref[h] + jnp.sum(p, axis=-1, keepdims=True)
            acc_ref[h] = alpha * acc_ref[h] + _dot(p.astype(BF16), v_ref[keys, head(h)])
        return carry

    lax.fori_loop(0, (qi + MOBA_CHUNK - 1) // MOBA_CHUNK, body, 0)
    for h in heads:
        o_ref[:, head(h)] = _head_rms(acc_ref[h] / l_ref[h], g_ref[:, head(h)]).astype(BF16)


def _moba_prompt(q, kv_bf16, kmean, g, n_batch, t):
    rows = q.shape[0]
    tq = MOBA_BLOCK
    nb = t // tq
    hps = MOBA_HEADS_PER_STEP
    groups = N_HEADS_MB // hps
    width = hps * HEAD_DIM
    assert nb <= LANES and nb % MOBA_CHUNK == 0 and N_HEADS_MB % hps == 0
    return pl.pallas_call(
        functools.partial(_moba_prompt_kernel, nb=nb),
        grid=(n_batch, groups, nb),
        in_specs=[
            pl.BlockSpec((tq, width), lambda b, h, i: (b * nb + i, h)),
            pl.BlockSpec((t, width), lambda b, h, i: (b, h)),
            pl.BlockSpec((t, width), lambda b, h, i: (b, groups + h)),
            pl.BlockSpec((1, nb, width), lambda b, h, i: (b, 0, h)),
            pl.BlockSpec((1, width), lambda b, h, i: (0, h)),
        ],
        out_specs=pl.BlockSpec((tq, width), lambda b, h, i: (b * nb + i, h)),
        out_shape=jax.ShapeDtypeStruct((rows, D_MB), BF16),
        scratch_shapes=[pltpu.VMEM((hps, t, HEAD_DIM + LANES), BF16),
                        pltpu.VMEM((hps, tq, HEAD_DIM + LANES), BF16),
                        pltpu.VMEM((hps, tq, 1), F32),
                        pltpu.VMEM((hps, tq, 1), F32),
                        pltpu.VMEM((hps, tq, HEAD_DIM), F32)],
        compiler_params=_cparams(("arbitrary", "arbitrary", "arbitrary")),
        name="moba_prompt",
    )(q, kv_bf16, kv_bf16, kmean, g)


def _head_match(m_rows, n_cols, nh):
    r = lax.broadcasted_iota(jnp.int32, (m_rows, n_cols), 0)
    c = lax.broadcasted_iota(jnp.int32, (m_rows, n_cols), 1)
    return (r % nh) == (c % nh), r // nh, c // nh


def _sb_rows(q, k_rows, v_rows, c, acc, tri, valid):
    m_rows, n = q.shape[0], k_rows.shape[0]
    z = _dot_nt(q, k_rows) * SCALE
    sp = _softplus(z)
    l1mb = jnp.where(valid, -sp, 0.0)
    tiles = [l1mb[:, j * LANES:(j + 1) * LANES] for j in range(n // LANES)]
    later = [None] * len(tiles)
    for j in reversed(range(len(tiles))):
        later[j] = c
        c = c + jnp.sum(tiles[j], axis=-1, keepdims=True)
    hi, lo = _split_bf16(jnp.concatenate(tiles, axis=0))
    within = _dot(hi, tri) + _dot(lo, tri)
    after = jnp.concatenate(
        [within[j * m_rows:(j + 1) * m_rows] + later[j] for j in range(len(tiles))], axis=1)
    w = jnp.where(valid, jnp.exp(z - sp + after), 0.0)
    return c, acc + _dot(w.astype(BF16), v_rows)


def _sb_sample_kernel(pt_ref, q_ref, kn_ref, vn_ref, g_ref, cache_ref, o_ref,
                      buf_ref, newk_ref, newv_ref, sem, *, n_pages):
    b = pl.program_id(0)
    nh = N_HEADS_SB
    m_rows = q_ref.shape[1]
    n_new = kn_ref.shape[1]
    page_rows = PAGE_SIZE * nh

    def page_copy(p, slot, seq=b):
        return pltpu.make_async_copy(cache_ref.at[pt_ref[seq, p]], buf_ref.at[slot], sem.at[slot])

    @pl.when(b == 0)
    def _():
        page_copy(n_pages - 1, 0).start()

    newk_ref[...] = jnp.zeros(newk_ref.shape, F32)
    newv_ref[...] = jnp.zeros(newv_ref.shape, F32)
    newk_ref[0:n_new, :] = kn_ref[0]
    newv_ref[0:n_new, :] = vn_ref[0]

    q = q_ref[0]
    tri = _tri(LANES)

    match, qt, kt = _head_match(m_rows, LANES, nh)
    c, acc = _sb_rows(q, newk_ref[...].astype(BF16), newv_ref[...].astype(BF16),
                      jnp.zeros((m_rows, 1), F32), jnp.zeros((m_rows, HEAD_DIM), F32),
                      tri, match & (kt < qt))
    page_match, _, _ = _head_match(m_rows, page_rows, nh)

    def cond(st):
        p, c, _ = st
        return (p >= 0) & (jnp.max(c) > SB_SKIP_BELOW)

    def body(st):
        p, c, acc = st
        slot = (n_pages - 1 - p) % 2
        page_copy(p, slot).wait()

        @pl.when(p >= 1)
        def _():
            page_copy(p - 1, 1 - slot).start()

        k_rows = buf_ref[slot, :, 0].reshape(page_rows, HEAD_DIM).astype(BF16)
        v_rows = buf_ref[slot, :, 1].reshape(page_rows, HEAD_DIM).astype(BF16)
        c, acc = _sb_rows(q, k_rows, v_rows, c, acc, tri, page_match)
        return p - 1, c, acc

    p_end, _, acc = lax.while_loop(cond, body, (n_pages - 1, c, acc))

    @pl.when(p_end >= 0)
    def _():
        page_copy(p_end, (n_pages - 1 - p_end) % 2).wait()

    @pl.when(b + 1 < pl.num_programs(0))
    def _():
        page_copy(n_pages - 1, 0, seq=b + 1).start()

    o_ref[0] = _head_rms(acc, g_ref[...]).astype(BF16)


def _sb_sample(page_table, q, k_new, v_new, g_rows, cache):
    nb, n_pages = page_table.shape
    m_rows, n_new = q.shape[1], k_new.shape[1]
    nh = cache.shape[3]
    assert n_new <= LANES and m_rows % 8 == 0
    grid_spec = pltpu.PrefetchScalarGridSpec(
        num_scalar_prefetch=1,
        grid=(nb,),
        in_specs=[
            pl.BlockSpec((1, m_rows, HEAD_DIM), lambda b, pt: (b, 0, 0)),
            pl.BlockSpec((1, n_new, HEAD_DIM), lambda b, pt: (b, 0, 0)),
            pl.BlockSpec((1, n_new, HEAD_DIM), lambda b, pt: (b, 0, 0)),
            pl.BlockSpec((m_rows, HEAD_DIM), lambda b, pt: (0, 0)),
            pl.BlockSpec(memory_space=pl.ANY),
        ],
        out_specs=pl.BlockSpec((1, m_rows, HEAD_DIM), lambda b, pt: (b, 0, 0)),
        scratch_shapes=[pltpu.VMEM((2, PAGE_SIZE, 2, nh, HEAD_DIM), F32),
                        pltpu.VMEM((LANES, HEAD_DIM), F32),
                        pltpu.VMEM((LANES, HEAD_DIM), F32),
                        pltpu.SemaphoreType.DMA((2,))],
    )
    return pl.pallas_call(
        functools.partial(_sb_sample_kernel, n_pages=n_pages),
        grid_spec=grid_spec,
        out_shape=jax.ShapeDtypeStruct((nb, m_rows, HEAD_DIM), BF16),
        compiler_params=_cparams(("arbitrary",)),
        name="sb_sample",
    )(page_table, q, k_new, v_new, g_rows, cache)


MOBA_PAGES_PER_STEP = 16


def _moba_sample_kernel(pt_ref, q_ref, kn_ref, vn_ref, g_ref, *rest, n_pages):
    page_refs = rest[:MOBA_PAGES_PER_STEP]
    o_ref, newk_ref, newv_ref, m_ref, l_ref, s_ref, oall_ref = rest[MOBA_PAGES_PER_STEP:]
    step = pl.program_id(1)
    nh = N_HEADS_MB
    m_rows = q_ref.shape[1]
    n_new = kn_ref.shape[1]
    page_rows = PAGE_SIZE * nh
    ppb = MOBA_BLOCK // PAGE_SIZE
    bps = MOBA_PAGES_PER_STEP // ppb

    qh, ql = _split_bf16(q_ref[0])
    qq = jnp.concatenate([qh, ql], axis=0)

    def partial_softmax(k_rows, v_rows, valid):
        qk2 = _dot_nt(qq, k_rows)
        qk = qk2[0:m_rows] + qk2[m_rows:2 * m_rows]
        s = jnp.where(valid, qk * SCALE, NEG)
        m = jnp.max(s, axis=-1, keepdims=True)
        e = jnp.exp(s - m)
        l = jnp.sum(e, axis=-1, keepdims=True)
        o = _dot(e.astype(BF16), v_rows)
        return jnp.sum(jnp.where(valid, qk, 0.0), axis=-1, keepdims=True), m, l, o

    lane = lax.broadcasted_iota(jnp.int32, (m_rows, LANES), 1)

    @pl.when(step == 0)
    def _():
        m_ref[...] = jnp.full(m_ref.shape, NEG, F32)
        l_ref[...] = jnp.zeros(l_ref.shape, F32)
        s_ref[...] = jnp.zeros(s_ref.shape, F32)

    block_match, _, _ = _head_match(m_rows, ppb * page_rows, nh)

    def block_rows(i, kv):
        return jnp.concatenate([r[0, :, kv].reshape(page_rows, HEAD_DIM).astype(BF16)
                                for r in page_refs[i * ppb:(i + 1) * ppb]], axis=0)

    qk2 = [_dot_nt(qq, block_rows(i, 0)) for i in range(bps)]
    es = []
    m_all, l_all, s_all = m_ref[...], l_ref[...], s_ref[...]
    for i in range(bps):
        qk = qk2[i][0:m_rows] + qk2[i][m_rows:2 * m_rows]
        s = jnp.where(block_match, qk * SCALE, NEG)
        m = jnp.max(s, axis=-1, keepdims=True)
        e = jnp.exp(s - m)
        es.append(e.astype(BF16))
        here = lane == step * bps + i
        m_all = jnp.where(here, m, m_all)
        l_all = jnp.where(here, jnp.sum(e, axis=-1, keepdims=True), l_all)
        s_all = jnp.where(here, jnp.sum(jnp.where(block_match, qk, 0.0), axis=-1, keepdims=True), s_all)
    m_ref[...], l_ref[...], s_ref[...] = m_all, l_all, s_all
    for i in range(bps):
        oall_ref[step * bps + i] = _dot(es[i], block_rows(i, 1))

    @pl.when(step == pl.num_programs(1) - 1)
    def _():
        n_blocks = n_pages // ppb
        sel = _topk_mask(s_ref[...] * (1.0 / MOBA_BLOCK), n_blocks)

        newk_ref[...] = jnp.zeros(newk_ref.shape, F32)
        newv_ref[...] = jnp.zeros(newv_ref.shape, F32)
        newk_ref[0:n_new, :] = kn_ref[0]
        newv_ref[0:n_new, :] = vn_ref[0]
        match, qt, kt = _head_match(m_rows, LANES, nh)
        _, m_own, l_own, o_own = partial_softmax(newk_ref[...].astype(BF16), newv_ref[...].astype(BF16),
                                                 match & (kt <= qt))

        m_all = m_ref[...]
        m_tot = jnp.maximum(jnp.max(jnp.where(sel > 0.5, m_all, NEG), axis=-1, keepdims=True), m_own)
        wgt = jnp.where(sel > 0.5, jnp.exp(m_all - m_tot), 0.0)
        w_own = jnp.exp(m_own - m_tot)
        l_tot = jnp.sum(wgt * l_ref[...], axis=-1, keepdims=True) + w_own * l_own

        def merge(n, acc):
            w = jnp.sum(jnp.where(lane == n, wgt, 0.0), axis=-1, keepdims=True)
            return acc + w * oall_ref[n]

        acc = lax.fori_loop(0, n_blocks, merge, w_own * o_own)
        o_ref[0] = _head_rms(acc / l_tot, g_ref[...]).astype(BF16)


def _moba_sample(page_table, q, k_new, v_new, g_rows, cache):
    nb, n_pages = page_table.shape
    m_rows, n_new = q.shape[1], k_new.shape[1]
    nh = cache.shape[3]
    pps = MOBA_PAGES_PER_STEP
    assert n_pages <= LANES and MOBA_BLOCK == 2 * PAGE_SIZE and n_new <= LANES
    assert pps % 2 == 0 and n_pages % pps == 0
    page_spec = lambda i: pl.BlockSpec((1, PAGE_SIZE, 2, nh, HEAD_DIM),
                                       lambda b, s, pt: (pt[b, s * pps + i], 0, 0, 0, 0))
    grid_spec = pltpu.PrefetchScalarGridSpec(
        num_scalar_prefetch=1,
        grid=(nb, n_pages // pps),
        in_specs=[
            pl.BlockSpec((1, m_rows, HEAD_DIM), lambda b, s, pt: (b, 0, 0)),
            pl.BlockSpec((1, n_new, HEAD_DIM), lambda b, s, pt: (b, 0, 0)),
            pl.BlockSpec((1, n_new, HEAD_DIM), lambda b, s, pt: (b, 0, 0)),
            pl.BlockSpec((m_rows, HEAD_DIM), lambda b, s, pt: (0, 0)),
        ] + [page_spec(i) for i in range(pps)],
        out_specs=pl.BlockSpec((1, m_rows, HEAD_DIM), lambda b, s, pt: (b, 0, 0)),
        scratch_shapes=[pltpu.VMEM((LANES, HEAD_DIM), F32),
                        pltpu.VMEM((LANES, HEAD_DIM), F32),
                        pltpu.VMEM((m_rows, LANES), F32),
                        pltpu.VMEM((m_rows, LANES), F32),
                        pltpu.VMEM((m_rows, LANES), F32),
                        pltpu.VMEM((n_pages * PAGE_SIZE // MOBA_BLOCK, m_rows, HEAD_DIM), F32)],
    )
    return pl.pallas_call(
        functools.partial(_moba_sample_kernel, n_pages=n_pages),
        grid_spec=grid_spec,
        out_shape=jax.ShapeDtypeStruct((nb, m_rows, HEAD_DIM), BF16),
        compiler_params=_cparams(("arbitrary", "arbitrary")),
        name="moba_sample",
    )(page_table, q, k_new, v_new, g_rows, *([cache] * pps))


def _out_proj_kernel(osb_ref, omb_ref, x_ref, g0_ref, b0_ref, wa_ref, wb_ref, g_ref, b_ref, h1_ref, h1b_ref,
                     *, alpha):
    mix = _dot(osb_ref[...], wa_ref[...]) + _dot(omb_ref[...], wb_ref[...])
    h = _layer_norm(x_ref[...], g0_ref[...], b0_ref[...])
    h1 = _layer_norm(alpha * h + mix, g_ref[...], b_ref[...])
    h1_ref[...] = h1
    h1b_ref[...] = h1.astype(BF16)


def _out_proj(o_sb, o_mb, x, ln_in_g, ln_in_b, w_out_bf16, g, b, tm, alpha):
    rows, d = x.shape
    row_spec = pl.BlockSpec((1, d), lambda i: (0, 0))
    return pl.pallas_call(
        functools.partial(_out_proj_kernel, alpha=alpha),
        grid=(rows // tm,),
        in_specs=[
            pl.BlockSpec((tm, D_SB), lambda i: (i, 0)),
            pl.BlockSpec((tm, D_MB), lambda i: (i, 0)),
            pl.BlockSpec((tm, d), lambda i: (i, 0)),
            row_spec,
            row_spec,
            pl.BlockSpec((D_SB, d), lambda i: (0, 0)),
            pl.BlockSpec((D_MB, d), lambda i: (1, 0)),
            row_spec,
            row_spec,
        ],
        out_specs=[pl.BlockSpec((tm, d), lambda i: (i, 0)), pl.BlockSpec((tm, d), lambda i: (i, 0))],
        out_shape=[jax.ShapeDtypeStruct((rows, d), F32), jax.ShapeDtypeStruct((rows, d), BF16)],
        compiler_params=_cparams(("arbitrary",)),
        name="out_proj",
    )(o_sb, o_mb, x, ln_in_g, ln_in_b, w_out_bf16, w_out_bf16, g, b)


HALO = 16


def _conv_ffn_kernel(*refs, alpha, seq_len, tail, has_state):
    if has_state:
        (h1_ref, h1b_ref, halo_ref, wg_ref, wu_ref, wd_ref, cw_ref, cb_ref, g_ref, b_ref, p1_ref, p2_ref,
         out_ref, gt_ref, x_ref, gbuf_ref, acc_ref) = refs
    else:
        (h1_ref, h1b_ref, halo_ref, wg_ref, wu_ref, wd_ref, cw_ref, cb_ref, g_ref, b_ref,
         out_ref, gt_ref, x_ref, gbuf_ref, acc_ref) = refs
    i = pl.program_id(0)
    j = pl.program_id(1)
    tm = h1b_ref.shape[0]
    te = h1_ref.shape[0]
    n_f = pl.num_programs(1) - tm // te

    @pl.when(j == 0)
    def _():
        x_ref[0:HALO, :] = halo_ref[...]
        x_ref[HALO:HALO + tm, :] = h1b_ref[...]
        acc_ref[...] = jnp.zeros(acc_ref.shape, F32)

    @pl.when(j < n_f)
    def _():
        gbuf_ref[...] = _dot(x_ref[...], wg_ref[...])
        u = _dot(x_ref[HALO:HALO + tm, :], wu_ref[...])
        g0 = gbuf_ref[HALO:HALO + tm, :]
        g1 = gbuf_ref[HALO - 1:HALO - 1 + tm, :]
        g2 = gbuf_ref[HALO - 2:HALO - 2 + tm, :]
        pos = (i * tm + lax.broadcasted_iota(jnp.int32, (tm, 1), 0)) % seq_len
        g1 = jnp.where(pos >= 1, g1, p1_ref[...] if has_state else 0.0)
        g2 = jnp.where(pos >= 2, g2, p2_ref[...] if has_state else 0.0)
        cw = cw_ref[...]
        c = cb_ref[...] + cw[0:1, :] * g2 + cw[1:2, :] * g1 + cw[2:3, :] * g0
        a = 0.5 * c * (1.0 + lax.erf(c * math.sqrt(0.5))) * u
        acc_ref[...] += _dot(a.astype(BF16), wd_ref[...])
        gt_ref[0] = g0[tm - tail:, :]

    @pl.when(j >= n_f)
    def _():
        rows = pl.ds(pl.multiple_of((j - n_f) * te, te), te)
        out_ref[...] = _layer_norm(alpha * h1_ref[...] + acc_ref[rows, :], g_ref[...], b_ref[...])


def _conv_ffn(h1, h1b, wg, wu, wd, conv_w, conv_b, g, b, tm, te, tf, seq_len, tail, alpha, state=None):
    rows, d = h1.shape
    f = wg.shape[1]
    has_state = state is not None
    n_f, n_e = f // tf, tm // te
    fcol = lambda j: jnp.minimum(j, n_f - 1)
    erow = lambda i, j: i * n_e + jnp.minimum(jnp.maximum(j - n_f, 0), n_e - 1)
    in_specs = [
        pl.BlockSpec((te, d), lambda i, j: (erow(i, j), 0)),
        pl.BlockSpec((tm, d), lambda i, j: (i, 0)),
        pl.BlockSpec((HALO, d), lambda i, j: (jnp.maximum(i * (tm // HALO) - 1, 0), 0)),
        pl.BlockSpec((d, tf), lambda i, j: (0, fcol(j))),
        pl.BlockSpec((d, tf), lambda i, j: (0, fcol(j))),
        pl.BlockSpec((tf, d), lambda i, j: (fcol(j), 0)),
        pl.BlockSpec((CONV_WIDTH, tf), lambda i, j: (0, fcol(j))),
        pl.BlockSpec((1, tf), lambda i, j: (0, fcol(j))),
        pl.BlockSpec((1, d), lambda i, j: (0, 0)),
        pl.BlockSpec((1, d), lambda i, j: (0, 0)),
    ]
    args = [h1, h1b, h1b, wg, wu, wd, conv_w, conv_b, g, b]
    if has_state:
        in_specs += [pl.BlockSpec((tm, tf), lambda i, j: (i, fcol(j))),
                     pl.BlockSpec((tm, tf), lambda i, j: (i, fcol(j)))]
        args += list(state)
    return pl.pallas_call(
        functools.partial(_conv_ffn_kernel, alpha=alpha, seq_len=seq_len, tail=tail, has_state=has_state),
        grid=(rows // tm, n_f + n_e),
        in_specs=in_specs,
        out_specs=[pl.BlockSpec((te, d), lambda i, j: (erow(i, j), 0)),
                   pl.BlockSpec((1, tail, tf), lambda i, j: (i, 0, fcol(j)))],
        out_shape=[jax.ShapeDtypeStruct((rows, d), F32),
                   jax.ShapeDtypeStruct((rows // tm, tail, f), F32)],
        scratch_shapes=[pltpu.VMEM((HALO + tm, d), BF16), pltpu.VMEM((HALO + tm, tf), F32),
                        pltpu.VMEM((tm, d), F32)],
        compiler_params=_cparams(("arbitrary", "arbitrary")),
        name="conv_ffn",
    )(*args)


def _rope_tables(pos):
    half = HEAD_DIM // 2
    inv = ROPE_THETA ** (-jnp.arange(half, dtype=F32) / half)
    ang = pos.astype(F32)[:, None] * inv[None, :]
    cos, sin = jnp.cos(ang), jnp.sin(ang)
    return jnp.concatenate([cos, cos], axis=-1), jnp.concatenate([-sin, sin], axis=-1)


def _row(v):
    return v.reshape(1, -1)


def kernel(x_prompt, x_sample, cache_kv_sb, cache_kv_mb, state_conv, page_table, ln_in_g, ln_in_b, w_in, g_sb, g_mb, w_out, ln1_g, ln1_b, w_gate, w_up, conv_w, conv_b, w_down, ln2_g, ln2_b):
    depth = w_in.shape[0]
    n_p, t_p, d = x_prompt.shape
    n_s, t_s, _ = x_sample.shape
    f = w_gate.shape[-1]
    past = page_table.shape[1] * PAGE_SIZE
    alpha = (2 * depth) ** 0.25
    assert depth == 1, "the input LayerNorm is fused into the first layer's projection"
    assert t_p % MOBA_BLOCK == 0 and past % MOBA_BLOCK == 0
    l = 0

    tm_p, tf = 512, 512
    tm_qkv = tm_ffn = 1024
    rows_p, rows_s = n_p * t_p, n_s * t_s
    xp, xs = x_prompt.reshape(rows_p, d), x_sample.reshape(rows_s, d)
    cos_p, sin_p = _rope_tables(jnp.arange(t_p, dtype=jnp.int32))
    cos_s, sin_s = _rope_tables(past + jnp.arange(t_s, dtype=jnp.int32))
    cos_s, sin_s = jnp.tile(cos_s, (n_s, 1)), jnp.tile(sin_s, (n_s, 1))

    ln_g, ln_b = _row(ln_in_g), _row(ln_in_b)
    w_in_b = w_in[l].astype(BF16)
    w_out_b = w_out[l].astype(BF16)
    wg_b, wu_b, wd_b = w_gate[l].astype(BF16), w_up[l].astype(BF16), w_down[l].astype(BF16)
    ffn_consts = (conv_w[l], _row(conv_b[l]), _row(ln2_g[l]), _row(ln2_b[l]))

    qsb, kvsb, qmb, kvmb, kvsb_bf, kvmb_bf, kmean = _ln_qkv(
        xp, ln_g, ln_b, cos_p, sin_p, w_in_b, tm_qkv, t_p // tm_qkv, True)
    o_sb = _sb_prompt(qsb, kvsb_bf, _row(g_sb[l]), n_p, t_p)
    o_mb = _moba_prompt(qmb, kvmb_bf, kmean.reshape(n_p, t_p // MOBA_BLOCK, D_MB), _row(g_mb[l]), n_p, t_p)
    h1, h1b = _out_proj(o_sb, o_mb, xp, ln_g, ln_b, w_out_b, _row(ln1_g[l]), _row(ln1_b[l]), tm_p, alpha)
    y_p, g_tail = _conv_ffn(h1, h1b, wg_b, wu_b, wd_b, *ffn_consts, tm_ffn, 256, tf, t_p, 8, alpha)
    kv_sb_p = kvsb.reshape(n_p, t_p, 2, N_HEADS_SB, HEAD_DIM)
    kv_mb_p = kvmb.reshape(n_p, t_p, 2, N_HEADS_MB, HEAD_DIM)
    tiles_per_seq = t_p // tm_ffn
    conv_p = g_tail[tiles_per_seq - 1::tiles_per_seq, 8 - (CONV_WIDTH - 1):, :]

    qsb, kvsb, qmb, kvmb = _ln_qkv(xs, ln_g, ln_b, cos_s, sin_s, w_in_b, rows_s, 1, False)
    kv_sb_s = kvsb.reshape(n_s, t_s, 2, N_HEADS_SB, HEAD_DIM)
    kv_mb_s = kvmb.reshape(n_s, t_s, 2, N_HEADS_MB, HEAD_DIM)
    o_sb = _sb_sample(page_table, qsb.reshape(n_s, t_s * N_HEADS_SB, HEAD_DIM),
                      kv_sb_s[:, :, 0].reshape(n_s, t_s * N_HEADS_SB, HEAD_DIM),
                      kv_sb_s[:, :, 1].reshape(n_s, t_s * N_HEADS_SB, HEAD_DIM),
                      jnp.tile(g_sb[l].reshape(N_HEADS_SB, HEAD_DIM), (t_s, 1)), cache_kv_sb[l])
    o_mb = _moba_sample(page_table, qmb.reshape(n_s, t_s * N_HEADS_MB, HEAD_DIM),
                        kv_mb_s[:, :, 0].reshape(n_s, t_s * N_HEADS_MB, HEAD_DIM),
                        kv_mb_s[:, :, 1].reshape(n_s, t_s * N_HEADS_MB, HEAD_DIM),
                        jnp.tile(g_mb[l].reshape(N_HEADS_MB, HEAD_DIM), (t_s, 1)), cache_kv_mb[l])
    h1, h1b = _out_proj(o_sb.reshape(rows_s, D_SB), o_mb.reshape(rows_s, D_MB), xs, ln_g, ln_b, w_out_b,
                        _row(ln1_g[l]), _row(ln1_b[l]), rows_s, alpha)
    sc = state_conv[l].astype(F32)
    assert t_s >= CONV_WIDTH - 1 == 2
    prev1 = jnp.pad(sc[:, 1:2], ((0, 0), (0, t_s - 1), (0, 0))).reshape(rows_s, f)
    prev2 = jnp.pad(sc, ((0, 0), (0, t_s - 2), (0, 0))).reshape(rows_s, f)
    y_s, g_all = _conv_ffn(h1, h1b, wg_b, wu_b, wd_b, *ffn_consts, rows_s, rows_s, tf, t_s, rows_s, alpha,
                           state=(prev1, prev2))
    conv_s = g_all.reshape(n_s, t_s, f)[:, t_s - (CONV_WIDTH - 1):]

    return (y_p.reshape(n_p, t_p, d), y_s.reshape(n_s, t_s, d), kv_sb_p[None], kv_mb_p[None], conv_p[None],
            kv_sb_s[None], kv_mb_s[None], conv_s[None])
```

```python
import functools
import math

import jax
import jax.numpy as jnp
from jax import lax
from jax.experimental import pallas as pl
from jax.experimental.pallas import tpu as pltpu

F32 = jnp.float32
BF16 = jnp.bfloat16

HEAD_DIM = 128
N_HEADS_SB = 8
N_HEADS_MB = 8
D_SB = N_HEADS_SB * HEAD_DIM
D_MB = N_HEADS_MB * HEAD_DIM
MOBA_BLOCK = 256
MOBA_TOPK = 3
ROPE_THETA = 10000.0
CONV_WIDTH = 3
PAGE_SIZE = 128
LN_EPS = 1e-5
RMS_EPS = 1e-6
NEG = -1e30
SCALE = 1.0 / math.sqrt(HEAD_DIM)

SB_SKIP_BELOW = -105.0

SB_TQ = 128
LANES = 128
VMEM_LIMIT_V7X = 56 * 1024 * 1024


def _cparams(sem):
    return pltpu.CompilerParams(dimension_semantics=sem, vmem_limit_bytes=VMEM_LIMIT_V7X)


def _layer_norm(x, g, b):
    mu = jnp.mean(x, axis=-1, keepdims=True)
    xc = x - mu
    var = jnp.mean(xc * xc, axis=-1, keepdims=True)
    return xc * lax.rsqrt(var + LN_EPS) * g + b


def _head_rms(o, g):
    ms = jnp.mean(o * o, axis=-1, keepdims=True)
    return o * lax.rsqrt(ms + RMS_EPS) * g


def _dot_nt(a, b):
    return lax.dot_general(a, b, (((1,), (1,)), ((), ())), preferred_element_type=F32)


def _dot(a, b):
    return jnp.dot(a, b, preferred_element_type=F32)


def _split_bf16(x):
    hi = x.astype(BF16)
    lo = (x - hi.astype(F32)).astype(BF16)
    return hi, lo


def _softplus(z):
    return jnp.maximum(z, 0.0) + jnp.log(1.0 + jnp.exp(-jnp.abs(z)))


def _tri(n):
    r = lax.broadcasted_iota(jnp.int32, (n, n), 0)
    s = lax.broadcasted_iota(jnp.int32, (n, n), 1)
    return (r > s).astype(BF16)


def _ln_qkv_kernel(x_ref, g_ref, b_ref, cos_ref, sin_ref, w_ref, qsb_ref, kvsb_ref, qmb_ref, kvmb_ref,
                   *rest, attention_copies, steps_per_seg):
    if attention_copies:
        kvsb_bf_ref, kvmb_bf_ref, kmean_ref, hb_ref = rest
    else:
        (hb_ref,) = rest
    j = pl.program_id(1)

    @pl.when(j == 0)
    def _():
        hb_ref[...] = _layer_norm(x_ref[...], g_ref[...], b_ref[...]).astype(BF16)

    seg = j // steps_per_seg

    def project():
        return _dot(hb_ref[...], w_ref[...])

    def rope(a):
        cos = cos_ref[...]
        sin = sin_ref[...]
        heads = []
        for hh in range(a.shape[1] // HEAD_DIM):
            ah = a[:, hh * HEAD_DIM:(hh + 1) * HEAD_DIM]
            heads.append(ah * cos + pltpu.roll(ah, HEAD_DIM // 2, axis=1) * sin)
        return jnp.concatenate(heads, axis=1)

    @pl.when(seg == 0)
    def _():
        qsb_ref[...] = project().astype(BF16)

    @pl.when((seg == 1) | (seg == 2))
    def _():
        acc = project()
        kvsb_ref[...] = acc
        if attention_copies:
            kvsb_bf_ref[...] = acc.astype(BF16)

    @pl.when(seg == 3)
    def _():
        qmb_ref[...] = rope(project())

    @pl.when(seg == 4)
    def _():
        k = rope(project())
        kvmb_ref[...] = k
        if attention_copies:
            kvmb_bf_ref[...] = k.astype(BF16)
            for r in range(k.shape[0] // MOBA_BLOCK):
                kmean_ref[0, r:r + 1, :] = jnp.mean(k[r * MOBA_BLOCK:(r + 1) * MOBA_BLOCK, :], axis=0,
                                                    keepdims=True)

    @pl.when(seg == 5)
    def _():
        acc = project()
        kvmb_ref[...] = acc
        if attention_copies:
            kvmb_bf_ref[...] = acc.astype(BF16)


def _ln_qkv(x, g, b, cos, sin, w_bf16, tm, tn, n_pos_tiles, attention_copies):
    rows, d = x.shape
    assert w_bf16.shape[1] == 3 * D_SB + 3 * D_MB and D_SB == D_MB and D_SB % tn == 0
    sps = D_SB // tn
    grid = (rows // tm, 6 * sps)
    col = lambda j, first_seg, n_seg: jnp.minimum(jnp.maximum(j - first_seg * sps, 0), n_seg * sps - 1)
    out_specs = [
        pl.BlockSpec((tm, tn), lambda i, j: (i, col(j, 0, 1))),
        pl.BlockSpec((tm, tn), lambda i, j: (i, col(j, 1, 2))),
        pl.BlockSpec((tm, tn), lambda i, j: (i, col(j, 3, 1))),
        pl.BlockSpec((tm, tn), lambda i, j: (i, col(j, 4, 2))),
    ]
    out_shape = [
        jax.ShapeDtypeStruct((rows, D_SB), BF16),
        jax.ShapeDtypeStruct((rows, 2 * D_SB), F32),
        jax.ShapeDtypeStruct((rows, D_MB), F32),
        jax.ShapeDtypeStruct((rows, 2 * D_MB), F32),
    ]
    if attention_copies:
        assert tm % MOBA_BLOCK == 0
        bpt = tm // MOBA_BLOCK
        out_specs += [
            pl.BlockSpec((tm, tn), lambda i, j: (i, col(j, 1, 2))),
            pl.BlockSpec((tm, tn), lambda i, j: (i, col(j, 4, 2))),
            pl.BlockSpec((1, bpt, tn), lambda i, j: (i, 0, col(j, 4, 1))),
        ]
        out_shape += [
            jax.ShapeDtypeStruct((rows, 2 * D_SB), BF16),
            jax.ShapeDtypeStruct((rows, 2 * D_MB), BF16),
            jax.ShapeDtypeStruct((rows // tm, bpt, D_MB), F32),
        ]
    return pl.pallas_call(
        functools.partial(_ln_qkv_kernel, attention_copies=attention_copies, steps_per_seg=sps),
        grid=grid,
        in_specs=[
            pl.BlockSpec((tm, d), lambda i, j: (i, 0)),
            pl.BlockSpec((1, d), lambda i, j: (0, 0)),
            pl.BlockSpec((1, d), lambda i, j: (0, 0)),
            pl.BlockSpec((tm, HEAD_DIM), lambda i, j: (i % n_pos_tiles, 0)),
            pl.BlockSpec((tm, HEAD_DIM), lambda i, j: (i % n_pos_tiles, 0)),
            pl.BlockSpec((d, tn), lambda i, j: (0, j)),
        ],
        out_specs=out_specs,
        out_shape=out_shape,
        scratch_shapes=[pltpu.VMEM((tm, d), BF16)],
        compiler_params=_cparams(("arbitrary", "arbitrary")),
        name="ln_qkv",
    )(x, g, b, cos, sin, w_bf16)


def _sb_prompt_kernel(q_ref, kv_ref, g_ref, o_ref, c_ref, acc_ref):
    qi = pl.program_id(1)
    nh, tq = N_HEADS_SB, SB_TQ
    tri = _tri(tq)
    row = lax.broadcasted_iota(jnp.int32, (nh * tq, tq), 0) % tq
    col = lax.broadcasted_iota(jnp.int32, (nh * tq, tq), 1)

    def head(h):
        return slice(h * HEAD_DIM, (h + 1) * HEAD_DIM)

    def attend(kb, valid):
        keys = pl.ds(pl.multiple_of(kb * tq, tq), tq)
        z = jnp.concatenate([_dot_nt(q_ref[:, head(h)], kv_ref[keys, head(h)]) for h in range(nh)],
                            axis=0) * SCALE
        sp = _softplus(z)
        l1mb = -sp if valid is None else jnp.where(valid, -sp, 0.0)
        hi, lo = _split_bf16(l1mb)
        c = c_ref[...]
        after = _dot(hi, tri) + _dot(lo, tri) + c
        w = jnp.exp(z - sp + after)
        if valid is not None:
            w = jnp.where(valid, w, 0.0)
        wb = w.astype(BF16)
        acc_ref[...] += jnp.concatenate(
            [_dot(wb[h * tq:(h + 1) * tq, :], kv_ref[keys, head(nh + h)]) for h in range(nh)], axis=0)
        c = c + jnp.sum(l1mb, axis=-1, keepdims=True)
        c_ref[...] = c
        return jnp.max(c)

    c_ref[...] = jnp.zeros(c_ref.shape, F32)
    acc_ref[...] = jnp.zeros(acc_ref.shape, F32)
    c_max = attend(qi, col < row)

    def cond(st):
        kb, c_max = st
        return (kb >= 0) & (c_max > SB_SKIP_BELOW)

    def body(st):
        kb, _ = st
        return kb - 1, attend(kb, None)

    lax.while_loop(cond, body, (qi - 1, c_max))
    for h in range(nh):
        o_ref[:, head(h)] = _head_rms(acc_ref[h * tq:(h + 1) * tq, :], g_ref[:, head(h)]).astype(BF16)


def _sb_prompt(q, kv_bf16, g, n_batch, t):
    rows = q.shape[0]
    nq = t // SB_TQ
    return pl.pallas_call(
        _sb_prompt_kernel,
        grid=(n_batch, nq),
        in_specs=[
            pl.BlockSpec((SB_TQ, D_SB), lambda b, i: (b * nq + i, 0)),
            pl.BlockSpec((t, 2 * D_SB), lambda b, i: (b, 0)),
            pl.BlockSpec((1, D_SB), lambda b, i: (0, 0)),
        ],
        out_specs=pl.BlockSpec((SB_TQ, D_SB), lambda b, i: (b * nq + i, 0)),
        out_shape=jax.ShapeDtypeStruct((rows, D_SB), BF16),
        scratch_shapes=[pltpu.VMEM((N_HEADS_SB * SB_TQ, 1), F32),
                        pltpu.VMEM((N_HEADS_SB * SB_TQ, HEAD_DIM), F32)],
        compiler_params=_cparams(("arbitrary", "arbitrary")),
        name="sb_prompt",
    )(q, kv_bf16, g)


def _topk_mask(gate, n_valid, axis=1):
    idx = lax.broadcasted_iota(jnp.int32, gate.shape, axis)
    idxf = idx.astype(F32)
    g = jnp.where(idx < n_valid, gate, NEG)
    sel = jnp.zeros(gate.shape, F32)
    for _ in range(MOBA_TOPK):
        m = jnp.max(g, axis=axis, keepdims=True)
        first = jnp.min(jnp.where(g == m, idxf, float(gate.shape[axis])), axis=axis, keepdims=True)
        pick = (idxf == first) & (m > 0.5 * NEG)
        sel = jnp.where(pick, 1.0, sel)
        g = jnp.where(pick, NEG, g)
    return sel


MOBA_CHUNK = 4
MOBA_HEADS_PER_STEP = 4
LOG2E = 1.4426950408889634


def _moba_prompt_kernel(q_ref, k_ref, v_ref, kmean_ref, g_ref, o_ref,
                        kaug_ref, qaug_ref, m_ref, l_ref, acc_ref, *, nb):
    qi = pl.program_id(2)
    tq = MOBA_BLOCK
    t = k_ref.shape[0]
    heads = range(MOBA_HEADS_PER_STEP)

    def head(h):
        return slice(h * HEAD_DIM, (h + 1) * HEAD_DIM)

    @pl.when(qi == 0)
    def _():
        blk = lax.broadcasted_iota(jnp.int32, (t, LANES), 0) // tq
        onehot = (blk == lax.broadcasted_iota(jnp.int32, (t, LANES), 1)).astype(BF16)
        for h in heads:
            kaug_ref[h, :, 0:HEAD_DIM] = k_ref[:, head(h)]
            kaug_ref[h, :, HEAD_DIM:HEAD_DIM + LANES] = onehot

    row = lax.broadcasted_iota(jnp.int32, (tq, tq), 0)
    col = lax.broadcasted_iota(jnp.int32, (tq, tq), 1)
    own = pl.ds(pl.multiple_of(qi * tq, tq), tq)

    for h in heads:
        qh = q_ref[:, head(h)].astype(BF16)
        gate_t = _dot_nt(kmean_ref[0, :, head(h)].astype(BF16), qh)
        sel_t = _topk_mask(gate_t, qi, axis=0)
        bias_t = jnp.concatenate([jnp.where(sel_t > 0.5, 0.0, NEG), jnp.full((LANES - nb, tq), NEG, F32)], axis=0)
        qaug_ref[h, :, 0:HEAD_DIM] = qh
        qaug_ref[h, :, HEAD_DIM:HEAD_DIM + LANES] = bias_t.T.astype(BF16)

        s = jnp.where(col <= row, _dot_nt(qh, k_ref[own, head(h)]) * (SCALE * LOG2E), NEG)
        m = jnp.max(s, axis=-1, keepdims=True)
        p = jnp.exp2(s - m)
        m_ref[h] = m
        l_ref[h] = jnp.sum(p, axis=-1, keepdims=True)
        acc_ref[h] = _dot(p.astype(BF16), v_ref[own, head(h)])

    def body(j, carry):
        keys = pl.ds(pl.multiple_of(j * (MOBA_CHUNK * tq), MOBA_CHUNK * tq), MOBA_CHUNK * tq)
        for h in heads:
            s = _dot_nt(qaug_ref[h], kaug_ref[h, keys, :]) * (SCALE * LOG2E)
            m = m_ref[h]
            m_new = jnp.maximum(m, jnp.max(s, axis=-1, keepdims=True))
            alpha = jnp.exp2(m - m_new)
            p = jnp.exp2(s - m_new)
            m_ref[h] = m_new
            l_ref[h] = alpha * l_ref[h] + jnp.sum(p, axis=-1, keepdims=True)
            acc_ref[h] = alpha * acc_ref[h] + _dot(p.astype(BF16), v_ref[keys, head(h)])
        return carry

    lax.fori_loop(0, (qi + MOBA_CHUNK - 1) // MOBA_CHUNK, body, 0)
    for h in heads:
        o_ref[:, head(h)] = _head_rms(acc_ref[h] / l_ref[h], g_ref[:, head(h)]).astype(BF16)


def _moba_prompt(q, kv_bf16, kmean, g, n_batch, t):
    rows = q.shape[0]
    tq = MOBA_BLOCK
    nb = t // tq
    hps = MOBA_HEADS_PER_STEP
    groups = N_HEADS_MB // hps
    width = hps * HEAD_DIM
    assert nb <= LANES and nb % MOBA_CHUNK == 0 and N_HEADS_MB % hps == 0
    return pl.pallas_call(
        functools.partial(_moba_prompt_kernel, nb=nb),
        grid=(n_batch, groups, nb),
        in_specs=[
            pl.BlockSpec((tq, width), lambda b, h, i: (b * nb + i, h)),
            pl.BlockSpec((t, width), lambda b, h, i: (b, h)),
            pl.BlockSpec((t, width), lambda b, h, i: (b, groups + h)),
            pl.BlockSpec((1, nb, width), lambda b, h, i: (b, 0, h)),
            pl.BlockSpec((1, width), lambda b, h, i: (0, h)),
        ],
        out_specs=pl.BlockSpec((tq, width), lambda b, h, i: (b * nb + i, h)),
        out_shape=jax.ShapeDtypeStruct((rows, D_MB), BF16),
        scratch_shapes=[pltpu.VMEM((hps, t, HEAD_DIM + LANES), BF16),
                        pltpu.VMEM((hps, tq, HEAD_DIM + LANES), BF16),
                        pltpu.VMEM((hps, tq, 1), F32),
                        pltpu.VMEM((hps, tq, 1), F32),
                        pltpu.VMEM((hps, tq, HEAD_DIM), F32)],
        compiler_params=_cparams(("arbitrary", "arbitrary", "arbitrary")),
        name="moba_prompt",
    )(q, kv_bf16, kv_bf16, kmean, g)


def _head_match(m_rows, n_cols, nh):
    r = lax.broadcasted_iota(jnp.int32, (m_rows, n_cols), 0)
    c = lax.broadcasted_iota(jnp.int32, (m_rows, n_cols), 1)
    return (r % nh) == (c % nh), r // nh, c // nh


def _sb_rows(q, k_rows, v_rows, c, acc, tri, valid):
    m_rows, n = q.shape[0], k_rows.shape[0]
    z = _dot_nt(q, k_rows) * SCALE
    sp = _softplus(z)
    l1mb = jnp.where(valid, -sp, 0.0)
    tiles = [l1mb[:, j * LANES:(j + 1) * LANES] for j in range(n // LANES)]
    later = [None] * len(tiles)
    for j in reversed(range(len(tiles))):
        later[j] = c
        c = c + jnp.sum(tiles[j], axis=-1, keepdims=True)
    hi, lo = _split_bf16(jnp.concatenate(tiles, axis=0))
    within = _dot(hi, tri) + _dot(lo, tri)
    after = jnp.concatenate(
        [within[j * m_rows:(j + 1) * m_rows] + later[j] for j in range(len(tiles))], axis=1)
    w = jnp.where(valid, jnp.exp(z - sp + after), 0.0)
    return c, acc + _dot(w.astype(BF16), v_rows)


def _sb_sample_kernel(pt_ref, q_ref, kn_ref, vn_ref, g_ref, cache_ref, o_ref,
                      buf_ref, newk_ref, newv_ref, sem, *, n_pages):
    b = pl.program_id(0)
    nh = N_HEADS_SB
    m_rows = q_ref.shape[1]
    n_new = kn_ref.shape[1]
    page_rows = PAGE_SIZE * nh

    def page_copy(p, slot, seq=b):
        return pltpu.make_async_copy(cache_ref.at[pt_ref[seq, p]], buf_ref.at[slot], sem.at[slot])

    @pl.when(b == 0)
    def _():
        page_copy(n_pages - 1, 0).start()

    newk_ref[...] = jnp.zeros(newk_ref.shape, F32)
    newv_ref[...] = jnp.zeros(newv_ref.shape, F32)
    newk_ref[0:n_new, :] = kn_ref[0]
    newv_ref[0:n_new, :] = vn_ref[0]

    q = q_ref[0]
    tri = _tri(LANES)

    match, qt, kt = _head_match(m_rows, LANES, nh)
    c, acc = _sb_rows(q, newk_ref[...].astype(BF16), newv_ref[...].astype(BF16),
                      jnp.zeros((m_rows, 1), F32), jnp.zeros((m_rows, HEAD_DIM), F32),
                      tri, match & (kt < qt))
    page_match, _, _ = _head_match(m_rows, page_rows, nh)

    def cond(st):
        p, c, _ = st
        return (p >= 0) & (jnp.max(c) > SB_SKIP_BELOW)

    def body(st):
        p, c, acc = st
        slot = (n_pages - 1 - p) % 2
        page_copy(p, slot).wait()

        @pl.when(p >= 1)
        def _():
            page_copy(p - 1, 1 - slot).start()

        k_rows = buf_ref[slot, :, 0].reshape(page_rows, HEAD_DIM).astype(BF16)
        v_rows = buf_ref[slot, :, 1].reshape(page_rows, HEAD_DIM).astype(BF16)
        c, acc = _sb_rows(q, k_rows, v_rows, c, acc, tri, page_match)
        return p - 1, c, acc

    p_end, _, acc = lax.while_loop(cond, body, (n_pages - 1, c, acc))

    @pl.when(p_end >= 0)
    def _():
        page_copy(p_end, (n_pages - 1 - p_end) % 2).wait()

    @pl.when(b + 1 < pl.num_programs(0))
    def _():
        page_copy(n_pages - 1, 0, seq=b + 1).start()

    o_ref[0] = _head_rms(acc, g_ref[...]).astype(BF16)


def _sb_sample(page_table, q, k_new, v_new, g_rows, cache):
    nb, n_pages = page_table.shape
    m_rows, n_new = q.shape[1], k_new.shape[1]
    nh = cache.shape[3]
    assert n_new <= LANES and m_rows % 8 == 0
    grid_spec = pltpu.PrefetchScalarGridSpec(
        num_scalar_prefetch=1,
        grid=(nb,),
        in_specs=[
            pl.BlockSpec((1, m_rows, HEAD_DIM), lambda b, pt: (b, 0, 0)),
            pl.BlockSpec((1, n_new, HEAD_DIM), lambda b, pt: (b, 0, 0)),
            pl.BlockSpec((1, n_new, HEAD_DIM), lambda b, pt: (b, 0, 0)),
            pl.BlockSpec((m_rows, HEAD_DIM), lambda b, pt: (0, 0)),
            pl.BlockSpec(memory_space=pl.ANY),
        ],
        out_specs=pl.BlockSpec((1, m_rows, HEAD_DIM), lambda b, pt: (b, 0, 0)),
        scratch_shapes=[pltpu.VMEM((2, PAGE_SIZE, 2, nh, HEAD_DIM), F32),
                        pltpu.VMEM((LANES, HEAD_DIM), F32),
                        pltpu.VMEM((LANES, HEAD_DIM), F32),
                        pltpu.SemaphoreType.DMA((2,))],
    )
    return pl.pallas_call(
        functools.partial(_sb_sample_kernel, n_pages=n_pages),
        grid_spec=grid_spec,
        out_shape=jax.ShapeDtypeStruct((nb, m_rows, HEAD_DIM), BF16),
        compiler_params=_cparams(("arbitrary",)),
        name="sb_sample",
    )(page_table, q, k_new, v_new, g_rows, cache)


MOBA_PAGES_PER_STEP = 16


def _moba_sample_kernel(pt_ref, q_ref, kn_ref, vn_ref, g_ref, *rest, n_pages):
    page_refs = rest[:MOBA_PAGES_PER_STEP]
    o_ref, newk_ref, newv_ref, m_ref, l_ref, s_ref, oall_ref = rest[MOBA_PAGES_PER_STEP:]
    step = pl.program_id(1)
    nh = N_HEADS_MB
    m_rows = q_ref.shape[1]
    n_new = kn_ref.shape[1]
    page_rows = PAGE_SIZE * nh
    ppb = MOBA_BLOCK // PAGE_SIZE
    bps = MOBA_PAGES_PER_STEP // ppb

    qh, ql = _split_bf16(q_ref[0])
    qq = jnp.concatenate([qh, ql], axis=0)

    def partial_softmax(k_rows, v_rows, valid):
        qk2 = _dot_nt(qq, k_rows)
        qk = qk2[0:m_rows] + qk2[m_rows:2 * m_rows]
        s = jnp.where(valid, qk * SCALE, NEG)
        m = jnp.max(s, axis=-1, keepdims=True)
        e = jnp.exp(s - m)
        l = jnp.sum(e, axis=-1, keepdims=True)
        o = _dot(e.astype(BF16), v_rows)
        return jnp.sum(jnp.where(valid, qk, 0.0), axis=-1, keepdims=True), m, l, o

    lane = lax.broadcasted_iota(jnp.int32, (m_rows, LANES), 1)

    @pl.when(step == 0)
    def _():
        m_ref[...] = jnp.full(m_ref.shape, NEG, F32)
        l_ref[...] = jnp.zeros(l_ref.shape, F32)
        s_ref[...] = jnp.zeros(s_ref.shape, F32)

    block_match, _, _ = _head_match(m_rows, ppb * page_rows, nh)

    def block_rows(i, kv):
        return jnp.concatenate([r[0, :, kv].reshape(page_rows, HEAD_DIM).astype(BF16)
                                for r in page_refs[i * ppb:(i + 1) * ppb]], axis=0)

    qk2 = [_dot_nt(qq, block_rows(i, 0)) for i in range(bps)]
    es = []
    m_all, l_all, s_all = m_ref[...], l_ref[...], s_ref[...]
    for i in range(bps):
        qk = qk2[i][0:m_rows] + qk2[i][m_rows:2 * m_rows]
        s = jnp.where(block_match, qk * SCALE, NEG)
        m = jnp.max(s, axis=-1, keepdims=True)
        e = jnp.exp(s - m)
        es.append(e.astype(BF16))
        here = lane == step * bps + i
        m_all = jnp.where(here, m, m_all)
        l_all = jnp.where(here, jnp.sum(e, axis=-1, keepdims=True), l_all)
        s_all = jnp.where(here, jnp.sum(jnp.where(block_match, qk, 0.0), axis=-1, keepdims=True), s_all)
    m_ref[...], l_ref[...], s_ref[...] = m_all, l_all, s_all
    for i in range(bps):
        oall_ref[step * bps + i] = _dot(es[i], block_rows(i, 1))

    @pl.when(step == pl.num_programs(1) - 1)
    def _():
        n_blocks = n_pages // ppb
        sel = _topk_mask(s_ref[...] * (1.0 / MOBA_BLOCK), n_blocks)

        newk_ref[...] = jnp.zeros(newk_ref.shape, F32)
        newv_ref[...] = jnp.zeros(newv_ref.shape, F32)
        newk_ref[0:n_new, :] = kn_ref[0]
        newv_ref[0:n_new, :] = vn_ref[0]
        match, qt, kt = _head_match(m_rows, LANES, nh)
        _, m_own, l_own, o_own = partial_softmax(newk_ref[...].astype(BF16), newv_ref[...].astype(BF16),
                                                 match & (kt <= qt))

        m_all = m_ref[...]
        m_tot = jnp.maximum(jnp.max(jnp.where(sel > 0.5, m_all, NEG), axis=-1, keepdims=True), m_own)
        wgt = jnp.where(sel > 0.5, jnp.exp(m_all - m_tot), 0.0)
        w_own = jnp.exp(m_own - m_tot)
        l_tot = jnp.sum(wgt * l_ref[...], axis=-1, keepdims=True) + w_own * l_own

        acc = w_own * o_own
        for n in range(n_blocks):
            acc = acc + wgt[:, n:n + 1] * oall_ref[n]
        o_ref[0] = _head_rms(acc / l_tot, g_ref[...]).astype(BF16)


def _moba_sample(page_table, q, k_new, v_new, g_rows, cache):
    nb, n_pages = page_table.shape
    m_rows, n_new = q.shape[1], k_new.shape[1]
    nh = cache.shape[3]
    pps = MOBA_PAGES_PER_STEP
    assert n_pages <= LANES and MOBA_BLOCK == 2 * PAGE_SIZE and n_new <= LANES
    assert pps % 2 == 0 and n_pages % pps == 0
    page_spec = lambda i: pl.BlockSpec((1, PAGE_SIZE, 2, nh, HEAD_DIM),
                                       lambda b, s, pt: (pt[b, s * pps + i], 0, 0, 0, 0))
    grid_spec = pltpu.PrefetchScalarGridSpec(
        num_scalar_prefetch=1,
        grid=(nb, n_pages // pps),
        in_specs=[
            pl.BlockSpec((1, m_rows, HEAD_DIM), lambda b, s, pt: (b, 0, 0)),
            pl.BlockSpec((1, n_new, HEAD_DIM), lambda b, s, pt: (b, 0, 0)),
            pl.BlockSpec((1, n_new, HEAD_DIM), lambda b, s, pt: (b, 0, 0)),
            pl.BlockSpec((m_rows, HEAD_DIM), lambda b, s, pt: (0, 0)),
        ] + [page_spec(i) for i in range(pps)],
        out_specs=pl.BlockSpec((1, m_rows, HEAD_DIM), lambda b, s, pt: (b, 0, 0)),
        scratch_shapes=[pltpu.VMEM((LANES, HEAD_DIM), F32),
                        pltpu.VMEM((LANES, HEAD_DIM), F32),
                        pltpu.VMEM((m_rows, LANES), F32),
                        pltpu.VMEM((m_rows, LANES), F32),
                        pltpu.VMEM((m_rows, LANES), F32),
                        pltpu.VMEM((n_pages * PAGE_SIZE // MOBA_BLOCK, m_rows, HEAD_DIM), F32)],
    )
    return pl.pallas_call(
        functools.partial(_moba_sample_kernel, n_pages=n_pages),
        grid_spec=grid_spec,
        out_shape=jax.ShapeDtypeStruct((nb, m_rows, HEAD_DIM), BF16),
        compiler_params=_cparams(("arbitrary", "arbitrary")),
        name="moba_sample",
    )(page_table, q, k_new, v_new, g_rows, *([cache] * pps))


def _out_proj_kernel(osb_ref, omb_ref, x_ref, g0_ref, b0_ref, wa_ref, wb_ref, g_ref, b_ref, h1_ref, h1b_ref,
                     *, alpha):
    mix = _dot(osb_ref[...], wa_ref[...]) + _dot(omb_ref[...], wb_ref[...])
    h = _layer_norm(x_ref[...], g0_ref[...], b0_ref[...])
    h1 = _layer_norm(alpha * h + mix, g_ref[...], b_ref[...])
    h1_ref[...] = h1
    h1b_ref[...] = h1.astype(BF16)


def _out_proj(o_sb, o_mb, x, ln_in_g, ln_in_b, w_out_bf16, g, b, tm, alpha):
    rows, d = x.shape
    row_spec = pl.BlockSpec((1, d), lambda i: (0, 0))
    return pl.pallas_call(
        functools.partial(_out_proj_kernel, alpha=alpha),
        grid=(rows // tm,),
        in_specs=[
            pl.BlockSpec((tm, D_SB), lambda i: (i, 0)),
            pl.BlockSpec((tm, D_MB), lambda i: (i, 0)),
            pl.BlockSpec((tm, d), lambda i: (i, 0)),
            row_spec,
            row_spec,
            pl.BlockSpec((D_SB, d), lambda i: (0, 0)),
            pl.BlockSpec((D_MB, d), lambda i: (1, 0)),
            row_spec,
            row_spec,
        ],
        out_specs=[pl.BlockSpec((tm, d), lambda i: (i, 0)), pl.BlockSpec((tm, d), lambda i: (i, 0))],
        out_shape=[jax.ShapeDtypeStruct((rows, d), F32), jax.ShapeDtypeStruct((rows, d), BF16)],
        compiler_params=_cparams(("arbitrary",)),
        name="out_proj",
    )(o_sb, o_mb, x, ln_in_g, ln_in_b, w_out_bf16, w_out_bf16, g, b)


HALO = 16


def _conv_ffn_kernel(*refs, alpha, seq_len, tail, has_state):
    if has_state:
        (h1_ref, h1b_ref, halo_ref, wg_ref, wu_ref, wd_ref, cw_ref, cb_ref, g_ref, b_ref, p1_ref, p2_ref,
         out_ref, gt_ref, x_ref, gbuf_ref, acc_ref) = refs
    else:
        (h1_ref, h1b_ref, halo_ref, wg_ref, wu_ref, wd_ref, cw_ref, cb_ref, g_ref, b_ref,
         out_ref, gt_ref, x_ref, gbuf_ref, acc_ref) = refs
    i = pl.program_id(0)
    j = pl.program_id(1)
    tm = h1b_ref.shape[0]
    te = h1_ref.shape[0]
    n_f = pl.num_programs(1) - (tm // te if te < tm else 0)

    @pl.when(j == 0)
    def _():
        x_ref[0:HALO, :] = halo_ref[...]
        x_ref[HALO:HALO + tm, :] = h1b_ref[...]
        acc_ref[...] = jnp.zeros(acc_ref.shape, F32)

    def d_ff_step():
        gbuf_ref[...] = _dot(x_ref[...], wg_ref[...])
        u = _dot(x_ref[HALO:HALO + tm, :], wu_ref[...])
        g0 = gbuf_ref[HALO:HALO + tm, :]
        g1 = gbuf_ref[HALO - 1:HALO - 1 + tm, :]
        g2 = gbuf_ref[HALO - 2:HALO - 2 + tm, :]
        pos = (i * tm + lax.broadcasted_iota(jnp.int32, (tm, 1), 0)) % seq_len
        g1 = jnp.where(pos >= 1, g1, p1_ref[...] if has_state else 0.0)
        g2 = jnp.where(pos >= 2, g2, p2_ref[...] if has_state else 0.0)
        cw = cw_ref[...]
        c = cb_ref[...] + cw[0:1, :] * g2 + cw[1:2, :] * g1 + cw[2:3, :] * g0
        a = 0.5 * c * (1.0 + lax.erf(c * math.sqrt(0.5))) * u
        acc_ref[...] += _dot(a.astype(BF16), wd_ref[...])
        gt_ref[0] = g0[tm - tail:, :]

    if te < tm:
        pl.when(j < n_f)(d_ff_step)

        @pl.when(j >= n_f)
        def _():
            rows = pl.ds(pl.multiple_of((j - n_f) * te, te), te)
            out_ref[...] = _layer_norm(alpha * h1_ref[...] + acc_ref[rows, :], g_ref[...], b_ref[...])
    else:
        d_ff_step()

        @pl.when(j == n_f - 1)
        def _():
            out_ref[...] = _layer_norm(alpha * h1_ref[...] + acc_ref[...], g_ref[...], b_ref[...])


def _conv_ffn(h1, h1b, wg, wu, wd, conv_w, conv_b, g, b, tm, te, tf, seq_len, tail, alpha, state=None):
    rows, d = h1.shape
    f = wg.shape[1]
    has_state = state is not None
    n_f = f // tf
    n_e = tm // te if te < tm else 0
    fcol = lambda j: jnp.minimum(j, n_f - 1)
    erow = (lambda i, j: i * n_e + jnp.minimum(jnp.maximum(j - n_f, 0), n_e - 1)) if n_e else (lambda i, j: i)
    in_specs = [
        pl.BlockSpec((te, d), lambda i, j: (erow(i, j), 0)),
        pl.BlockSpec((tm, d), lambda i, j: (i, 0)),
        pl.BlockSpec((HALO, d), lambda i, j: (jnp.maximum(i * (tm // HALO) - 1, 0), 0)),
        pl.BlockSpec((d, tf), lambda i, j: (0, fcol(j))),
        pl.BlockSpec((d, tf), lambda i, j: (0, fcol(j))),
        pl.BlockSpec((tf, d), lambda i, j: (fcol(j), 0)),
        pl.BlockSpec((CONV_WIDTH, tf), lambda i, j: (0, fcol(j))),
        pl.BlockSpec((1, tf), lambda i, j: (0, fcol(j))),
        pl.BlockSpec((1, d), lambda i, j: (0, 0)),
        pl.BlockSpec((1, d), lambda i, j: (0, 0)),
    ]
    args = [h1, h1b, h1b, wg, wu, wd, conv_w, conv_b, g, b]
    if has_state:
        in_specs += [pl.BlockSpec((tm, tf), lambda i, j: (i, fcol(j))),
                     pl.BlockSpec((tm, tf), lambda i, j: (i, fcol(j)))]
        args += list(state)
    return pl.pallas_call(
        functools.partial(_conv_ffn_kernel, alpha=alpha, seq_len=seq_len, tail=tail, has_state=has_state),
        grid=(rows // tm, n_f + n_e),
        in_specs=in_specs,
        out_specs=[pl.BlockSpec((te, d), lambda i, j: (erow(i, j), 0)),
                   pl.BlockSpec((1, tail, tf), lambda i, j: (i, 0, fcol(j)))],
        out_shape=[jax.ShapeDtypeStruct((rows, d), F32),
                   jax.ShapeDtypeStruct((rows // tm, tail, f), F32)],
        scratch_shapes=[pltpu.VMEM((HALO + tm, d), BF16), pltpu.VMEM((HALO + tm, tf), F32),
                        pltpu.VMEM((tm, d), F32)],
        compiler_params=_cparams(("arbitrary", "arbitrary")),
        name="conv_ffn",
    )(*args)


def _rope_tables(pos):
    half = HEAD_DIM // 2
    inv = ROPE_THETA ** (-jnp.arange(half, dtype=F32) / half)
    ang = pos.astype(F32)[:, None] * inv[None, :]
    cos, sin = jnp.cos(ang), jnp.sin(ang)
    return jnp.concatenate([cos, cos], axis=-1), jnp.concatenate([-sin, sin], axis=-1)


def _row(v):
    return v.reshape(1, -1)


def kernel(x_prompt, x_sample, cache_kv_sb, cache_kv_mb, state_conv, page_table, ln_in_g, ln_in_b, w_in, g_sb, g_mb, w_out, ln1_g, ln1_b, w_gate, w_up, conv_w, conv_b, w_down, ln2_g, ln2_b):
    depth = w_in.shape[0]
    n_p, t_p, d = x_prompt.shape
    n_s, t_s, _ = x_sample.shape
    f = w_gate.shape[-1]
    past = page_table.shape[1] * PAGE_SIZE
    alpha = (2 * depth) ** 0.25
    assert depth == 1, "the input LayerNorm is fused into the first layer's projection"
    assert t_p % MOBA_BLOCK == 0 and past % MOBA_BLOCK == 0
    l = 0

    tm_p, tf = 512, 512
    tm_qkv = 1024
    tm_ffn = 512
    rows_p, rows_s = n_p * t_p, n_s * t_s
    xp, xs = x_prompt.reshape(rows_p, d), x_sample.reshape(rows_s, d)
    cos_p, sin_p = _rope_tables(jnp.arange(t_p, dtype=jnp.int32))
    cos_s, sin_s = _rope_tables(past + jnp.arange(t_s, dtype=jnp.int32))
    cos_s, sin_s = jnp.tile(cos_s, (n_s, 1)), jnp.tile(sin_s, (n_s, 1))

    ln_g, ln_b = _row(ln_in_g), _row(ln_in_b)
    w_in_b = w_in[l].astype(BF16)
    w_out_b = w_out[l].astype(BF16)
    wg_b, wu_b, wd_b = w_gate[l].astype(BF16), w_up[l].astype(BF16), w_down[l].astype(BF16)
    ffn_consts = (conv_w[l], _row(conv_b[l]), _row(ln2_g[l]), _row(ln2_b[l]))

    qsb, kvsb, qmb, kvmb, kvsb_bf, kvmb_bf, kmean = _ln_qkv(
        xp, ln_g, ln_b, cos_p, sin_p, w_in_b, tm_qkv, 512, t_p // tm_qkv, True)
    o_sb = _sb_prompt(qsb, kvsb_bf, _row(g_sb[l]), n_p, t_p)
    o_mb = _moba_prompt(qmb, kvmb_bf, kmean.reshape(n_p, t_p // MOBA_BLOCK, D_MB), _row(g_mb[l]), n_p, t_p)
    h1, h1b = _out_proj(o_sb, o_mb, xp, ln_g, ln_b, w_out_b, _row(ln1_g[l]), _row(ln1_b[l]), tm_p, alpha)
    y_p, g_tail = _conv_ffn(h1, h1b, wg_b, wu_b, wd_b, *ffn_consts, tm_ffn, tm_ffn, tf, t_p, 8, alpha)
    kv_sb_p = kvsb.reshape(n_p, t_p, 2, N_HEADS_SB, HEAD_DIM)
    kv_mb_p = kvmb.reshape(n_p, t_p, 2, N_HEADS_MB, HEAD_DIM)
    tiles_per_seq = t_p // tm_ffn
    conv_p = g_tail[tiles_per_seq - 1::tiles_per_seq, 8 - (CONV_WIDTH - 1):, :]

    qsb, kvsb, qmb, kvmb = _ln_qkv(xs, ln_g, ln_b, cos_s, sin_s, w_in_b, rows_s, D_SB, 1, False)
    kv_sb_s = kvsb.reshape(n_s, t_s, 2, N_HEADS_SB, HEAD_DIM)
    kv_mb_s = kvmb.reshape(n_s, t_s, 2, N_HEADS_MB, HEAD_DIM)
    o_sb = _sb_sample(page_table, qsb.reshape(n_s, t_s * N_HEADS_SB, HEAD_DIM),
                      kv_sb_s[:, :, 0].reshape(n_s, t_s * N_HEADS_SB, HEAD_DIM),
                      kv_sb_s[:, :, 1].reshape(n_s, t_s * N_HEADS_SB, HEAD_DIM),
                      jnp.tile(g_sb[l].reshape(N_HEADS_SB, HEAD_DIM), (t_s, 1)), cache_kv_sb[l])
    o_mb = _moba_sample(page_table, qmb.reshape(n_s, t_s * N_HEADS_MB, HEAD_DIM),
                        kv_mb_s[:, :, 0].reshape(n_s, t_s * N_HEADS_MB, HEAD_DIM),
                        kv_mb_s[:, :, 1].reshape(n_s, t_s * N_HEADS_MB, HEAD_DIM),
                        jnp.tile(g_mb[l].reshape(N_HEADS_MB, HEAD_DIM), (t_s, 1)), cache_kv_mb[l])
    h1, h1b = _out_proj(o_sb.reshape(rows_s, D_SB), o_mb.reshape(rows_s, D_MB), xs, ln_g, ln_b, w_out_b,
                        _row(ln1_g[l]), _row(ln1_b[l]), rows_s, alpha)
    sc = state_conv[l].astype(F32)
    assert t_s >= CONV_WIDTH - 1 == 2
    prev1 = jnp.pad(sc[:, 1:2], ((0, 0), (0, t_s - 1), (0, 0))).reshape(rows_s, f)
    prev2 = jnp.pad(sc, ((0, 0), (0, t_s - 2), (0, 0))).reshape(rows_s, f)
    y_s, g_all = _conv_ffn(h1, h1b, wg_b, wu_b, wd_b, *ffn_consts, rows_s, rows_s, tf, t_s, rows_s, alpha,
                           state=(prev1, prev2))
    conv_s = g_all.reshape(n_s, t_s, f)[:, t_s - (CONV_WIDTH - 1):]

    return (y_p.reshape(n_p, t_p, d), y_s.reshape(n_s, t_s, d), kv_sb_p[None], kv_mb_p[None], conv_p[None],
            kv_sb_s[None], kv_mb_s[None], conv_s[None])
```

```python
import functools
import math

import jax
import jax.numpy as jnp
from jax import lax
from jax.experimental import pallas as pl
from jax.experimental.pallas import tpu as pltpu

F32 = jnp.float32
BF16 = jnp.bfloat16

HEAD_DIM = 128
N_HEADS_SB = 8
N_HEADS_MB = 8
D_SB = N_HEADS_SB * HEAD_DIM
D_MB = N_HEADS_MB * HEAD_DIM
MOBA_BLOCK = 256
MOBA_TOPK = 3
ROPE_THETA = 10000.0
CONV_WIDTH = 3
PAGE_SIZE = 128
LN_EPS = 1e-5
RMS_EPS = 1e-6
NEG = -1e30
SCALE = 1.0 / math.sqrt(HEAD_DIM)

SB_SKIP_BELOW = -105.0

SB_TQ = 128
LANES = 128
VMEM_LIMIT_V7X = 56 * 1024 * 1024


def _cparams(sem):
    return pltpu.CompilerParams(dimension_semantics=sem, vmem_limit_bytes=VMEM_LIMIT_V7X)


def _layer_norm(x, g, b):
    mu = jnp.mean(x, axis=-1, keepdims=True)
    xc = x - mu
    var = jnp.mean(xc * xc, axis=-1, keepdims=True)
    return xc * lax.rsqrt(var + LN_EPS) * g + b


def _head_rms(o, g):
    ms = jnp.mean(o * o, axis=-1, keepdims=True)
    return o * lax.rsqrt(ms + RMS_EPS) * g


def _dot_nt(a, b):
    return lax.dot_general(a, b, (((1,), (1,)), ((), ())), preferred_element_type=F32)


def _dot(a, b):
    return jnp.dot(a, b, preferred_element_type=F32)


def _split_bf16(x):
    hi = x.astype(BF16)
    lo = (x - hi.astype(F32)).astype(BF16)
    return hi, lo


def _softplus(z):
    return jnp.maximum(z, 0.0) + jnp.log(1.0 + jnp.exp(-jnp.abs(z)))


def _tri(n):
    r = lax.broadcasted_iota(jnp.int32, (n, n), 0)
    s = lax.broadcasted_iota(jnp.int32, (n, n), 1)
    return (r > s).astype(BF16)


def _ln_qkv_kernel(x_ref, g_ref, b_ref, cos_ref, sin_ref, w_ref, qsb_ref, kvsb_ref, qmb_ref, kvmb_ref,
                   *rest, attention_copies, steps_per_seg):
    if attention_copies:
        kvsb_bf_ref, kvmb_bf_ref, kmean_ref, hb_ref = rest
    else:
        (hb_ref,) = rest
    j = pl.program_id(1)

    @pl.when(j == 0)
    def _():
        hb_ref[...] = _layer_norm(x_ref[...], g_ref[...], b_ref[...]).astype(BF16)

    seg = j // steps_per_seg

    def project():
        return _dot(hb_ref[...], w_ref[...])

    def rope(a):
        cos = cos_ref[...]
        sin = sin_ref[...]
        heads = []
        for hh in range(a.shape[1] // HEAD_DIM):
            ah = a[:, hh * HEAD_DIM:(hh + 1) * HEAD_DIM]
            heads.append(ah * cos + pltpu.roll(ah, HEAD_DIM // 2, axis=1) * sin)
        return jnp.concatenate(heads, axis=1)

    @pl.when(seg == 0)
    def _():
        qsb_ref[...] = project().astype(BF16)

    @pl.when((seg == 1) | (seg == 2))
    def _():
        acc = project()
        kvsb_ref[...] = acc
        if attention_copies:
            kvsb_bf_ref[...] = acc.astype(BF16)

    @pl.when(seg == 3)
    def _():
        qmb_ref[...] = rope(project())

    @pl.when(seg == 4)
    def _():
        k = rope(project())
        kvmb_ref[...] = k
        if attention_copies:
            kvmb_bf_ref[...] = k.astype(BF16)
            for r in range(k.shape[0] // MOBA_BLOCK):
                kmean_ref[0, r:r + 1, :] = jnp.mean(k[r * MOBA_BLOCK:(r + 1) * MOBA_BLOCK, :], axis=0,
                                                    keepdims=True)

    @pl.when(seg == 5)
    def _():
        acc = project()
        kvmb_ref[...] = acc
        if attention_copies:
            kvmb_bf_ref[...] = acc.astype(BF16)


def _ln_qkv(x, g, b, cos, sin, w_bf16, tm, tn, n_pos_tiles, attention_copies):
    rows, d = x.shape
    assert w_bf16.shape[1] == 3 * D_SB + 3 * D_MB and D_SB == D_MB and D_SB % tn == 0
    sps = D_SB // tn
    grid = (rows // tm, 6 * sps)
    col = lambda j, first_seg, n_seg: jnp.minimum(jnp.maximum(j - first_seg * sps, 0), n_seg * sps - 1)
    out_specs = [
        pl.BlockSpec((tm, tn), lambda i, j: (i, col(j, 0, 1))),
        pl.BlockSpec((tm, tn), lambda i, j: (i, col(j, 1, 2))),
        pl.BlockSpec((tm, tn), lambda i, j: (i, col(j, 3, 1))),
        pl.BlockSpec((tm, tn), lambda i, j: (i, col(j, 4, 2))),
    ]
    out_shape = [
        jax.ShapeDtypeStruct((rows, D_SB), BF16),
        jax.ShapeDtypeStruct((rows, 2 * D_SB), F32),
        jax.ShapeDtypeStruct((rows, D_MB), F32),
        jax.ShapeDtypeStruct((rows, 2 * D_MB), F32),
    ]
    if attention_copies:
        assert tm % MOBA_BLOCK == 0
        bpt = tm // MOBA_BLOCK
        out_specs += [
            pl.BlockSpec((tm, tn), lambda i, j: (i, col(j, 1, 2))),
            pl.BlockSpec((tm, tn), lambda i, j: (i, col(j, 4, 2))),
            pl.BlockSpec((1, bpt, tn), lambda i, j: (i, 0, col(j, 4, 1))),
        ]
        out_shape += [
            jax.ShapeDtypeStruct((rows, 2 * D_SB), BF16),
            jax.ShapeDtypeStruct((rows, 2 * D_MB), BF16),
            jax.ShapeDtypeStruct((rows // tm, bpt, D_MB), F32),
        ]
    return pl.pallas_call(
        functools.partial(_ln_qkv_kernel, attention_copies=attention_copies, steps_per_seg=sps),
        grid=grid,
        in_specs=[
            pl.BlockSpec((tm, d), lambda i, j: (i, 0)),
            pl.BlockSpec((1, d), lambda i, j: (0, 0)),
            pl.BlockSpec((1, d), lambda i, j: (0, 0)),
            pl.BlockSpec((tm, HEAD_DIM), lambda i, j: (i % n_pos_tiles, 0)),
            pl.BlockSpec((tm, HEAD_DIM), lambda i, j: (i % n_pos_tiles, 0)),
            pl.BlockSpec((d, tn), lambda i, j: (0, j)),
        ],
        out_specs=out_specs,
        out_shape=out_shape,
        scratch_shapes=[pltpu.VMEM((tm, d), BF16)],
        compiler_params=_cparams(("arbitrary", "arbitrary")),
        name="ln_qkv",
    )(x, g, b, cos, sin, w_bf16)


def _sb_prompt_kernel(q_ref, kv_ref, g_ref, o_ref, c_ref, acc_ref):
    qi = pl.program_id(1)
    nh, tq = N_HEADS_SB, SB_TQ
    tri = _tri(tq)
    row = lax.broadcasted_iota(jnp.int32, (nh * tq, tq), 0) % tq
    col = lax.broadcasted_iota(jnp.int32, (nh * tq, tq), 1)

    def head(h):
        return slice(h * HEAD_DIM, (h + 1) * HEAD_DIM)

    def attend(kb, valid):
        keys = pl.ds(pl.multiple_of(kb * tq, tq), tq)
        z = jnp.concatenate([_dot_nt(q_ref[:, head(h)], kv_ref[keys, head(h)]) for h in range(nh)],
                            axis=0) * SCALE
        sp = _softplus(z)
        l1mb = -sp if valid is None else jnp.where(valid, -sp, 0.0)
        hi, lo = _split_bf16(l1mb)
        c = c_ref[...]
        after = _dot(hi, tri) + _dot(lo, tri) + c
        w = jnp.exp(z - sp + after)
        if valid is not None:
            w = jnp.where(valid, w, 0.0)
        wb = w.astype(BF16)
        acc_ref[...] += jnp.concatenate(
            [_dot(wb[h * tq:(h + 1) * tq, :], kv_ref[keys, head(nh + h)]) for h in range(nh)], axis=0)
        c = c + jnp.sum(l1mb, axis=-1, keepdims=True)
        c_ref[...] = c
        return jnp.max(c)

    c_ref[...] = jnp.zeros(c_ref.shape, F32)
    acc_ref[...] = jnp.zeros(acc_ref.shape, F32)
    c_max = attend(qi, col < row)

    def cond(st):
        kb, c_max = st
        return (kb >= 0) & (c_max > SB_SKIP_BELOW)

    def body(st):
        kb, _ = st
        return kb - 1, attend(kb, None)

    lax.while_loop(cond, body, (qi - 1, c_max))
    for h in range(nh):
        o_ref[:, head(h)] = _head_rms(acc_ref[h * tq:(h + 1) * tq, :], g_ref[:, head(h)]).astype(BF16)


def _sb_prompt(q, kv_bf16, g, n_batch, t):
    rows = q.shape[0]
    nq = t // SB_TQ
    return pl.pallas_call(
        _sb_prompt_kernel,
        grid=(n_batch, nq),
        in_specs=[
            pl.BlockSpec((SB_TQ, D_SB), lambda b, i: (b * nq + i, 0)),
            pl.BlockSpec((t, 2 * D_SB), lambda b, i: (b, 0)),
            pl.BlockSpec((1, D_SB), lambda b, i: (0, 0)),
        ],
        out_specs=pl.BlockSpec((SB_TQ, D_SB), lambda b, i: (b * nq + i, 0)),
        out_shape=jax.ShapeDtypeStruct((rows, D_SB), BF16),
        scratch_shapes=[pltpu.VMEM((N_HEADS_SB * SB_TQ, 1), F32),
                        pltpu.VMEM((N_HEADS_SB * SB_TQ, HEAD_DIM), F32)],
        compiler_params=_cparams(("arbitrary", "arbitrary")),
        name="sb_prompt",
    )(q, kv_bf16, g)


def _topk_mask(gate, n_valid, axis=1):
    idx = lax.broadcasted_iota(jnp.int32, gate.shape, axis)
    idxf = idx.astype(F32)
    g = jnp.where(idx < n_valid, gate, NEG)
    sel = jnp.zeros(gate.shape, F32)
    for _ in range(MOBA_TOPK):
        m = jnp.max(g, axis=axis, keepdims=True)
        first = jnp.min(jnp.where(g == m, idxf, float(gate.shape[axis])), axis=axis, keepdims=True)
        pick = (idxf == first) & (m > 0.5 * NEG)
        sel = jnp.where(pick, 1.0, sel)
        g = jnp.where(pick, NEG, g)
    return sel


MOBA_CHUNK = 4
MOBA_HEADS_PER_STEP = 4
LOG2E = 1.4426950408889634


def _moba_prompt_kernel(q_ref, k_ref, v_ref, kmean_ref, g_ref, o_ref,
                        kaug_ref, qaug_ref, m_ref, l_ref, acc_ref, *, nb):
    qi = pl.program_id(2)
    tq = MOBA_BLOCK
    t = k_ref.shape[0]
    heads = range(MOBA_HEADS_PER_STEP)

    def head(h):
        return slice(h * HEAD_DIM, (h + 1) * HEAD_DIM)

    @pl.when(qi == 0)
    def _():
        blk = lax.broadcasted_iota(jnp.int32, (t, LANES), 0) // tq
        onehot = (blk == lax.broadcasted_iota(jnp.int32, (t, LANES), 1)).astype(BF16)
        for h in heads:
            kaug_ref[h, :, 0:HEAD_DIM] = k_ref[:, head(h)]
            kaug_ref[h, :, HEAD_DIM:HEAD_DIM + LANES] = onehot

    row = lax.broadcasted_iota(jnp.int32, (tq, tq), 0)
    col = lax.broadcasted_iota(jnp.int32, (tq, tq), 1)
    own = pl.ds(pl.multiple_of(qi * tq, tq), tq)

    for h in heads:
        qh = q_ref[:, head(h)].astype(BF16)
        gate_t = _dot_nt(kmean_ref[0, :, head(h)].astype(BF16), qh)
        sel_t = _topk_mask(gate_t, qi, axis=0)
        bias_t = jnp.concatenate([jnp.where(sel_t > 0.5, 0.0, NEG), jnp.full((LANES - nb, tq), NEG, F32)], axis=0)
        qaug_ref[h, :, 0:HEAD_DIM] = qh
        qaug_ref[h, :, HEAD_DIM:HEAD_DIM + LANES] = bias_t.T.astype(BF16)

        s = jnp.where(col <= row, _dot_nt(qh, k_ref[own, head(h)]) * (SCALE * LOG2E), NEG)
        m = jnp.max(s, axis=-1, keepdims=True)
        p = jnp.exp2(s - m)
        m_ref[h] = m
        l_ref[h] = jnp.sum(p, axis=-1, keepdims=True)
        acc_ref[h] = _dot(p.astype(BF16), v_ref[own, head(h)])

    def body(j, carry):
        keys = pl.ds(pl.multiple_of(j * (MOBA_CHUNK * tq), MOBA_CHUNK * tq), MOBA_CHUNK * tq)
        for h in heads:
            s = _dot_nt(qaug_ref[h], kaug_ref[h, keys, :]) * (SCALE * LOG2E)
            m = m_ref[h]
            m_new = jnp.maximum(m, jnp.max(s, axis=-1, keepdims=True))
            alpha = jnp.exp2(m - m_new)
            p = jnp.exp2(s - m_new)
            m_ref[h] = m_new
            l_ref[h] = alpha * l_ref[h] + jnp.sum(p, axis=-1, keepdims=True)
            acc_ref[h] = alpha * acc_ref[h] + _dot(p.astype(BF16), v_ref[keys, head(h)])
        return carry

    lax.fori_loop(0, (qi + MOBA_CHUNK - 1) // MOBA_CHUNK, body, 0)
    for h in heads:
        o_ref[:, head(h)] = _head_rms(acc_ref[h] / l_ref[h], g_ref[:, head(h)]).astype(BF16)


def _moba_prompt(q, kv_bf16, kmean, g, n_batch, t):
    rows = q.shape[0]
    tq = MOBA_BLOCK
    nb = t // tq
    hps = MOBA_HEADS_PER_STEP
    groups = N_HEADS_MB // hps
    width = hps * HEAD_DIM
    assert nb <= LANES and nb % MOBA_CHUNK == 0 and N_HEADS_MB % hps == 0
    return pl.pallas_call(
        functools.partial(_moba_prompt_kernel, nb=nb),
        grid=(n_batch, groups, nb),
        in_specs=[
            pl.BlockSpec((tq, width), lambda b, h, i: (b * nb + i, h)),
            pl.BlockSpec((t, width), lambda b, h, i: (b, h)),
            pl.BlockSpec((t, width), lambda b, h, i: (b, groups + h)),
            pl.BlockSpec((1, nb, width), lambda b, h, i: (b, 0, h)),
            pl.BlockSpec((1, width), lambda b, h, i: (0, h)),
        ],
        out_specs=pl.BlockSpec((tq, width), lambda b, h, i: (b * nb + i, h)),
        out_shape=jax.ShapeDtypeStruct((rows, D_MB), BF16),
        scratch_shapes=[pltpu.VMEM((hps, t, HEAD_DIM + LANES), BF16),
                        pltpu.VMEM((hps, tq, HEAD_DIM + LANES), BF16),
                        pltpu.VMEM((hps, tq, 1), F32),
                        pltpu.VMEM((hps, tq, 1), F32),
                        pltpu.VMEM((hps, tq, HEAD_DIM), F32)],
        compiler_params=_cparams(("arbitrary", "arbitrary", "arbitrary")),
        name="moba_prompt",
    )(q, kv_bf16, kv_bf16, kmean, g)


def _head_match(m_rows, n_cols, nh):
    r = lax.broadcasted_iota(jnp.int32, (m_rows, n_cols), 0)
    c = lax.broadcasted_iota(jnp.int32, (m_rows, n_cols), 1)
    return (r % nh) == (c % nh), r // nh, c // nh


def _sb_rows(q, k_rows, v_rows, c, acc, tri, valid):
    m_rows, n = q.shape[0], k_rows.shape[0]
    z = _dot_nt(q, k_rows) * SCALE
    sp = _softplus(z)
    l1mb = jnp.where(valid, -sp, 0.0)
    tiles = [l1mb[:, j * LANES:(j + 1) * LANES] for j in range(n // LANES)]
    later = [None] * len(tiles)
    for j in reversed(range(len(tiles))):
        later[j] = c
        c = c + jnp.sum(tiles[j], axis=-1, keepdims=True)
    hi, lo = _split_bf16(jnp.concatenate(tiles, axis=0))
    within = _dot(hi, tri) + _dot(lo, tri)
    after = jnp.concatenate(
        [within[j * m_rows:(j + 1) * m_rows] + later[j] for j in range(len(tiles))], axis=1)
    w = jnp.where(valid, jnp.exp(z - sp + after), 0.0)
    return c, acc + _dot(w.astype(BF16), v_rows)


SB_PAGES_PER_TRIP = 2


def _sb_sample_kernel(pt_ref, q_ref, kn_ref, vn_ref, g_ref, cache_ref, o_ref,
                      buf_ref, newk_ref, newv_ref, sem, *, n_pages):
    b = pl.program_id(0)
    nh = N_HEADS_SB
    m_rows = q_ref.shape[1]
    n_new = kn_ref.shape[1]
    page_rows = PAGE_SIZE * nh
    group = SB_PAGES_PER_TRIP
    n_groups = n_pages // group

    def page_copy(gi, slot, k, seq=b):
        return pltpu.make_async_copy(cache_ref.at[pt_ref[seq, gi * group + k]], buf_ref.at[slot, k],
                                     sem.at[slot, k])

    def start_group(gi, slot, seq=b):
        for k in range(group):
            page_copy(gi, slot, k, seq).start()

    def wait_group(gi, slot):
        for k in range(group):
            page_copy(gi, slot, k).wait()

    @pl.when(b == 0)
    def _():
        start_group(n_groups - 1, 0)

    newk_ref[...] = jnp.zeros(newk_ref.shape, F32)
    newv_ref[...] = jnp.zeros(newv_ref.shape, F32)
    newk_ref[0:n_new, :] = kn_ref[0]
    newv_ref[0:n_new, :] = vn_ref[0]

    q = q_ref[0]
    tri = _tri(LANES)

    match, qt, kt = _head_match(m_rows, LANES, nh)
    c, acc = _sb_rows(q, newk_ref[...].astype(BF16), newv_ref[...].astype(BF16),
                      jnp.zeros((m_rows, 1), F32), jnp.zeros((m_rows, HEAD_DIM), F32),
                      tri, match & (kt < qt))
    group_match, _, _ = _head_match(m_rows, group * page_rows, nh)

    def cond(st):
        gi, c, _ = st
        return (gi >= 0) & (jnp.max(c) > SB_SKIP_BELOW)

    def body(st):
        gi, c, acc = st
        slot = (n_groups - 1 - gi) % 2
        wait_group(gi, slot)

        @pl.when(gi >= 1)
        def _():
            start_group(gi - 1, 1 - slot)

        def rows(kv):
            return jnp.concatenate([buf_ref[slot, k, :, kv].reshape(page_rows, HEAD_DIM).astype(BF16)
                                    for k in range(group)], axis=0)

        c, acc = _sb_rows(q, rows(0), rows(1), c, acc, tri, group_match)
        return gi - 1, c, acc

    g_end, _, acc = lax.while_loop(cond, body, (n_groups - 1, c, acc))

    @pl.when(g_end >= 0)
    def _():
        wait_group(g_end, (n_groups - 1 - g_end) % 2)

    @pl.when(b + 1 < pl.num_programs(0))
    def _():
        start_group(n_groups - 1, 0, seq=b + 1)

    o_ref[0] = _head_rms(acc, g_ref[...]).astype(BF16)


def _sb_sample(page_table, q, k_new, v_new, g_rows, cache):
    nb, n_pages = page_table.shape
    m_rows, n_new = q.shape[1], k_new.shape[1]
    nh = cache.shape[3]
    assert n_new <= LANES and m_rows % 8 == 0 and n_pages % SB_PAGES_PER_TRIP == 0
    grid_spec = pltpu.PrefetchScalarGridSpec(
        num_scalar_prefetch=1,
        grid=(nb,),
        in_specs=[
            pl.BlockSpec((1, m_rows, HEAD_DIM), lambda b, pt: (b, 0, 0)),
            pl.BlockSpec((1, n_new, HEAD_DIM), lambda b, pt: (b, 0, 0)),
            pl.BlockSpec((1, n_new, HEAD_DIM), lambda b, pt: (b, 0, 0)),
            pl.BlockSpec((m_rows, HEAD_DIM), lambda b, pt: (0, 0)),
            pl.BlockSpec(memory_space=pl.ANY),
        ],
        out_specs=pl.BlockSpec((1, m_rows, HEAD_DIM), lambda b, pt: (b, 0, 0)),
        scratch_shapes=[pltpu.VMEM((2, SB_PAGES_PER_TRIP, PAGE_SIZE, 2, nh, HEAD_DIM), F32),
                        pltpu.VMEM((LANES, HEAD_DIM), F32),
                        pltpu.VMEM((LANES, HEAD_DIM), F32),
                        pltpu.SemaphoreType.DMA((2, SB_PAGES_PER_TRIP))],
    )
    return pl.pallas_call(
        functools.partial(_sb_sample_kernel, n_pages=n_pages),
        grid_spec=grid_spec,
        out_shape=jax.ShapeDtypeStruct((nb, m_rows, HEAD_DIM), BF16),
        compiler_params=_cparams(("arbitrary",)),
        name="sb_sample",
    )(page_table, q, k_new, v_new, g_rows, cache)


MOBA_PAGES_PER_STEP = 16


def _moba_sample_kernel(pt_ref, q_ref, kn_ref, vn_ref, g_ref, *rest, n_pages):
    page_refs = rest[:MOBA_PAGES_PER_STEP]
    o_ref, newk_ref, newv_ref, m_ref, l_ref, s_ref, oall_ref = rest[MOBA_PAGES_PER_STEP:]
    step = pl.program_id(1)
    nh = N_HEADS_MB
    m_rows = q_ref.shape[1]
    n_new = kn_ref.shape[1]
    page_rows = PAGE_SIZE * nh
    ppb = MOBA_BLOCK // PAGE_SIZE
    bps = MOBA_PAGES_PER_STEP // ppb

    qh, ql = _split_bf16(q_ref[0])
    qq = jnp.concatenate([qh, ql], axis=0)

    def partial_softmax(k_rows, v_rows, valid):
        qk2 = _dot_nt(qq, k_rows)
        qk = qk2[0:m_rows] + qk2[m_rows:2 * m_rows]
        s = jnp.where(valid, qk * SCALE, NEG)
        m = jnp.max(s, axis=-1, keepdims=True)
        e = jnp.exp(s - m)
        l = jnp.sum(e, axis=-1, keepdims=True)
        o = _dot(e.astype(BF16), v_rows)
        return jnp.sum(jnp.where(valid, qk, 0.0), axis=-1, keepdims=True), m, l, o

    lane = lax.broadcasted_iota(jnp.int32, (m_rows, LANES), 1)

    @pl.when(step == 0)
    def _():
        m_ref[...] = jnp.full(m_ref.shape, NEG, F32)
        l_ref[...] = jnp.zeros(l_ref.shape, F32)
        s_ref[...] = jnp.zeros(s_ref.shape, F32)

    block_match, _, _ = _head_match(m_rows, ppb * page_rows, nh)

    def block_rows(i, kv):
        return jnp.concatenate([r[0, :, kv].reshape(page_rows, HEAD_DIM).astype(BF16)
                                for r in page_refs[i * ppb:(i + 1) * ppb]], axis=0)

    qk2 = [_dot_nt(qq, block_rows(i, 0)) for i in range(bps)]
    es = []
    m_all, l_all, s_all = m_ref[...], l_ref[...], s_ref[...]
    for i in range(bps):
        qk = qk2[i][0:m_rows] + qk2[i][m_rows:2 * m_rows]
        s = jnp.where(block_match, qk * SCALE, NEG)
        m = jnp.max(s, axis=-1, keepdims=True)
        e = jnp.exp(s - m)
        es.append(e.astype(BF16))
        here = lane == step * bps + i
        m_all = jnp.where(here, m, m_all)
        l_all = jnp.where(here, jnp.sum(e, axis=-1, keepdims=True), l_all)
        s_all = jnp.where(here, jnp.sum(jnp.where(block_match, qk, 0.0), axis=-1, keepdims=True), s_all)
    m_ref[...], l_ref[...], s_ref[...] = m_all, l_all, s_all
    for i in range(bps):
        oall_ref[step * bps + i] = _dot(es[i], block_rows(i, 1))

    @pl.when(step == pl.num_programs(1) - 1)
    def _():
        n_blocks = n_pages // ppb
        sel = _topk_mask(s_ref[...] * (1.0 / MOBA_BLOCK), n_blocks)

        newk_ref[...] = jnp.zeros(newk_ref.shape, F32)
        newv_ref[...] = jnp.zeros(newv_ref.shape, F32)
        newk_ref[0:n_new, :] = kn_ref[0]
        newv_ref[0:n_new, :] = vn_ref[0]
        match, qt, kt = _head_match(m_rows, LANES, nh)
        _, m_own, l_own, o_own = partial_softmax(newk_ref[...].astype(BF16), newv_ref[...].astype(BF16),
                                                 match & (kt <= qt))

        m_all = m_ref[...]
        m_tot = jnp.maximum(jnp.max(jnp.where(sel > 0.5, m_all, NEG), axis=-1, keepdims=True), m_own)
        wgt = jnp.where(sel > 0.5, jnp.exp(m_all - m_tot), 0.0)
        w_own = jnp.exp(m_own - m_tot)
        l_tot = jnp.sum(wgt * l_ref[...], axis=-1, keepdims=True) + w_own * l_own

        acc = w_own * o_own
        for n in range(n_blocks):
            acc = acc + wgt[:, n:n + 1] * oall_ref[n]
        o_ref[0] = _head_rms(acc / l_tot, g_ref[...]).astype(BF16)


def _moba_sample(page_table, q, k_new, v_new, g_rows, cache):
    nb, n_pages = page_table.shape
    m_rows, n_new = q.shape[1], k_new.shape[1]
    nh = cache.shape[3]
    pps = MOBA_PAGES_PER_STEP
    assert n_pages <= LANES and MOBA_BLOCK == 2 * PAGE_SIZE and n_new <= LANES
    assert pps % 2 == 0 and n_pages % pps == 0
    page_spec = lambda i: pl.BlockSpec((1, PAGE_SIZE, 2, nh, HEAD_DIM),
                                       lambda b, s, pt: (pt[b, s * pps + i], 0, 0, 0, 0))
    grid_spec = pltpu.PrefetchScalarGridSpec(
        num_scalar_prefetch=1,
        grid=(nb, n_pages // pps),
        in_specs=[
            pl.BlockSpec((1, m_rows, HEAD_DIM), lambda b, s, pt: (b, 0, 0)),
            pl.BlockSpec((1, n_new, HEAD_DIM), lambda b, s, pt: (b, 0, 0)),
            pl.BlockSpec((1, n_new, HEAD_DIM), lambda b, s, pt: (b, 0, 0)),
            pl.BlockSpec((m_rows, HEAD_DIM), lambda b, s, pt: (0, 0)),
        ] + [page_spec(i) for i in range(pps)],
        out_specs=pl.BlockSpec((1, m_rows, HEAD_DIM), lambda b, s, pt: (b, 0, 0)),
        scratch_shapes=[pltpu.VMEM((LANES, HEAD_DIM), F32),
                        pltpu.VMEM((LANES, HEAD_DIM), F32),
                        pltpu.VMEM((m_rows, LANES), F32),
                        pltpu.VMEM((m_rows, LANES), F32),
                        pltpu.VMEM((m_rows, LANES), F32),
                        pltpu.VMEM((n_pages * PAGE_SIZE // MOBA_BLOCK, m_rows, HEAD_DIM), F32)],
    )
    return pl.pallas_call(
        functools.partial(_moba_sample_kernel, n_pages=n_pages),
        grid_spec=grid_spec,
        out_shape=jax.ShapeDtypeStruct((nb, m_rows, HEAD_DIM), BF16),
        compiler_params=_cparams(("arbitrary", "arbitrary")),
        name="moba_sample",
    )(page_table, q, k_new, v_new, g_rows, *([cache] * pps))


def _out_proj_kernel(osb_ref, omb_ref, x_ref, g0_ref, b0_ref, wa_ref, wb_ref, g_ref, b_ref, h1_ref, h1b_ref,
                     *, alpha):
    mix = _dot(osb_ref[...], wa_ref[...]) + _dot(omb_ref[...], wb_ref[...])
    h = _layer_norm(x_ref[...], g0_ref[...], b0_ref[...])
    h1 = _layer_norm(alpha * h + mix, g_ref[...], b_ref[...])
    h1_ref[...] = h1
    h1b_ref[...] = h1.astype(BF16)


def _out_proj(o_sb, o_mb, x, ln_in_g, ln_in_b, w_out_bf16, g, b, tm, alpha):
    rows, d = x.shape
    row_spec = pl.BlockSpec((1, d), lambda i: (0, 0))
    return pl.pallas_call(
        functools.partial(_out_proj_kernel, alpha=alpha),
        grid=(rows // tm,),
        in_specs=[
            pl.BlockSpec((tm, D_SB), lambda i: (i, 0)),
            pl.BlockSpec((tm, D_MB), lambda i: (i, 0)),
            pl.BlockSpec((tm, d), lambda i: (i, 0)),
            row_spec,
            row_spec,
            pl.BlockSpec((D_SB, d), lambda i: (0, 0)),
            pl.BlockSpec((D_MB, d), lambda i: (1, 0)),
            row_spec,
            row_spec,
        ],
        out_specs=[pl.BlockSpec((tm, d), lambda i: (i, 0)), pl.BlockSpec((tm, d), lambda i: (i, 0))],
        out_shape=[jax.ShapeDtypeStruct((rows, d), F32), jax.ShapeDtypeStruct((rows, d), BF16)],
        compiler_params=_cparams(("arbitrary",)),
        name="out_proj",
    )(o_sb, o_mb, x, ln_in_g, ln_in_b, w_out_bf16, w_out_bf16, g, b)


HALO = 16


def _conv_ffn_kernel(*refs, alpha, seq_len, tail, has_state):
    if has_state:
        (h1_ref, h1b_ref, halo_ref, wg_ref, wu_ref, wd_ref, cw_ref, cb_ref, g_ref, b_ref, p1_ref, p2_ref,
         out_ref, gt_ref, x_ref, gbuf_ref, acc_ref) = refs
    else:
        (h1_ref, h1b_ref, halo_ref, wg_ref, wu_ref, wd_ref, cw_ref, cb_ref, g_ref, b_ref,
         out_ref, gt_ref, x_ref, gbuf_ref, acc_ref) = refs
    i = pl.program_id(0)
    j = pl.program_id(1)
    tm = h1b_ref.shape[0]
    te = h1_ref.shape[0]
    n_f = pl.num_programs(1) - (tm // te if te < tm else 0)

    @pl.when(j == 0)
    def _():
        x_ref[0:HALO, :] = halo_ref[...]
        x_ref[HALO:HALO + tm, :] = h1b_ref[...]
        acc_ref[...] = jnp.zeros(acc_ref.shape, F32)

    def d_ff_step():
        gbuf_ref[...] = _dot(x_ref[...], wg_ref[...])
        u = _dot(x_ref[HALO:HALO + tm, :], wu_ref[...])
        g0 = gbuf_ref[HALO:HALO + tm, :]
        g1 = gbuf_ref[HALO - 1:HALO - 1 + tm, :]
        g2 = gbuf_ref[HALO - 2:HALO - 2 + tm, :]
        pos = (i * tm + lax.broadcasted_iota(jnp.int32, (tm, 1), 0)) % seq_len
        g1 = jnp.where(pos >= 1, g1, p1_ref[...] if has_state else 0.0)
        g2 = jnp.where(pos >= 2, g2, p2_ref[...] if has_state else 0.0)
        cw = cw_ref[...]
        c = cb_ref[...] + cw[0:1, :] * g2 + cw[1:2, :] * g1 + cw[2:3, :] * g0
        a = 0.5 * c * (1.0 + lax.erf(c * math.sqrt(0.5))) * u
        acc_ref[...] += _dot(a.astype(BF16), wd_ref[...])
        gt_ref[0] = g0[tm - tail:, :]

    if te < tm:
        pl.when(j < n_f)(d_ff_step)

        @pl.when(j >= n_f)
        def _():
            rows = pl.ds(pl.multiple_of((j - n_f) * te, te), te)
            out_ref[...] = _layer_norm(alpha * h1_ref[...] + acc_ref[rows, :], g_ref[...], b_ref[...])
    else:
        d_ff_step()

        @pl.when(j == n_f - 1)
        def _():
            out_ref[...] = _layer_norm(alpha * h1_ref[...] + acc_ref[...], g_ref[...], b_ref[...])


def _conv_ffn(h1, h1b, wg, wu, wd, conv_w, conv_b, g, b, tm, te, tf, seq_len, tail, alpha, state=None):
    rows, d = h1.shape
    f = wg.shape[1]
    has_state = state is not None
    n_f = f // tf
    n_e = tm // te if te < tm else 0
    fcol = lambda j: jnp.minimum(j, n_f - 1)
    erow = (lambda i, j: i * n_e + jnp.minimum(jnp.maximum(j - n_f, 0), n_e - 1)) if n_e else (lambda i, j: i)
    in_specs = [
        pl.BlockSpec((te, d), lambda i, j: (erow(i, j), 0)),
        pl.BlockSpec((tm, d), lambda i, j: (i, 0)),
        pl.BlockSpec((HALO, d), lambda i, j: (jnp.maximum(i * (tm // HALO) - 1, 0), 0)),
        pl.BlockSpec((d, tf), lambda i, j: (0, fcol(j))),
        pl.BlockSpec((d, tf), lambda i, j: (0, fcol(j))),
        pl.BlockSpec((tf, d), lambda i, j: (fcol(j), 0)),
        pl.BlockSpec((CONV_WIDTH, tf), lambda i, j: (0, fcol(j))),
        pl.BlockSpec((1, tf), lambda i, j: (0, fcol(j))),
        pl.BlockSpec((1, d), lambda i, j: (0, 0)),
        pl.BlockSpec((1, d), lambda i, j: (0, 0)),
    ]
    args = [h1, h1b, h1b, wg, wu, wd, conv_w, conv_b, g, b]
    if has_state:
        in_specs += [pl.BlockSpec((tm, tf), lambda i, j: (i, fcol(j))),
                     pl.BlockSpec((tm, tf), lambda i, j: (i, fcol(j)))]
        args += list(state)
    return pl.pallas_call(
        functools.partial(_conv_ffn_kernel, alpha=alpha, seq_len=seq_len, tail=tail, has_state=has_state),
        grid=(rows // tm, n_f + n_e),
        in_specs=in_specs,
        out_specs=[pl.BlockSpec((te, d), lambda i, j: (erow(i, j), 0)),
                   pl.BlockSpec((1, tail, tf), lambda i, j: (i, 0, fcol(j)))],
        out_shape=[jax.ShapeDtypeStruct((rows, d), F32),
                   jax.ShapeDtypeStruct((rows // tm, tail, f), F32)],
        scratch_shapes=[pltpu.VMEM((HALO + tm, d), BF16), pltpu.VMEM((HALO + tm, tf), F32),
                        pltpu.VMEM((tm, d), F32)],
        compiler_params=_cparams(("arbitrary", "arbitrary")),
        name="conv_ffn",
    )(*args)


def _rope_tables(pos):
    half = HEAD_DIM // 2
    inv = ROPE_THETA ** (-jnp.arange(half, dtype=F32) / half)
    ang = pos.astype(F32)[:, None] * inv[None, :]
    cos, sin = jnp.cos(ang), jnp.sin(ang)
    return jnp.concatenate([cos, cos], axis=-1), jnp.concatenate([-sin, sin], axis=-1)


def _row(v):
    return v.reshape(1, -1)


def kernel(x_prompt, x_sample, cache_kv_sb, cache_kv_mb, state_conv, page_table, ln_in_g, ln_in_b, w_in, g_sb, g_mb, w_out, ln1_g, ln1_b, w_gate, w_up, conv_w, conv_b, w_down, ln2_g, ln2_b):
    depth = w_in.shape[0]
    n_p, t_p, d = x_prompt.shape
    n_s, t_s, _ = x_sample.shape
    f = w_gate.shape[-1]
    past = page_table.shape[1] * PAGE_SIZE
    alpha = (2 * depth) ** 0.25
    assert depth == 1, "the input LayerNorm is fused into the first layer's projection"
    assert t_p % MOBA_BLOCK == 0 and past % MOBA_BLOCK == 0
    l = 0

    tm_p, tf = 512, 512
    tm_qkv = 1024
    tm_ffn = 512
    rows_p, rows_s = n_p * t_p, n_s * t_s
    xp, xs = x_prompt.reshape(rows_p, d), x_sample.reshape(rows_s, d)
    cos_p, sin_p = _rope_tables(jnp.arange(t_p, dtype=jnp.int32))
    cos_s, sin_s = _rope_tables(past + jnp.arange(t_s, dtype=jnp.int32))
    cos_s, sin_s = jnp.tile(cos_s, (n_s, 1)), jnp.tile(sin_s, (n_s, 1))

    ln_g, ln_b = _row(ln_in_g), _row(ln_in_b)
    w_in_b = w_in[l].astype(BF16)
    w_out_b = w_out[l].astype(BF16)
    wg_b, wu_b, wd_b = w_gate[l].astype(BF16), w_up[l].astype(BF16), w_down[l].astype(BF16)
    ffn_consts = (conv_w[l], _row(conv_b[l]), _row(ln2_g[l]), _row(ln2_b[l]))

    qsb, kvsb, qmb, kvmb, kvsb_bf, kvmb_bf, kmean = _ln_qkv(
        xp, ln_g, ln_b, cos_p, sin_p, w_in_b, tm_qkv, 512, t_p // tm_qkv, True)
    o_sb = _sb_prompt(qsb, kvsb_bf, _row(g_sb[l]), n_p, t_p)
    o_mb = _moba_prompt(qmb, kvmb_bf, kmean.reshape(n_p, t_p // MOBA_BLOCK, D_MB), _row(g_mb[l]), n_p, t_p)
    h1, h1b = _out_proj(o_sb, o_mb, xp, ln_g, ln_b, w_out_b, _row(ln1_g[l]), _row(ln1_b[l]), tm_p, alpha)
    y_p, g_tail = _conv_ffn(h1, h1b, wg_b, wu_b, wd_b, *ffn_consts, tm_ffn, tm_ffn, tf, t_p, 8, alpha)
    kv_sb_p = kvsb.reshape(n_p, t_p, 2, N_HEADS_SB, HEAD_DIM)
    kv_mb_p = kvmb.reshape(n_p, t_p, 2, N_HEADS_MB, HEAD_DIM)
    tiles_per_seq = t_p // tm_ffn
    conv_p = g_tail[tiles_per_seq - 1::tiles_per_seq, 8 - (CONV_WIDTH - 1):, :]

    qsb, kvsb, qmb, kvmb = _ln_qkv(xs, ln_g, ln_b, cos_s, sin_s, w_in_b, rows_s, D_SB, 1, False)
    kv_sb_s = kvsb.reshape(n_s, t_s, 2, N_HEADS_SB, HEAD_DIM)
    kv_mb_s = kvmb.reshape(n_s, t_s, 2, N_HEADS_MB, HEAD_DIM)
    o_sb = _sb_sample(page_table, qsb.reshape(n_s, t_s * N_HEADS_SB, HEAD_DIM),
                      kv_sb_s[:, :, 0].reshape(n_s, t_s * N_HEADS_SB, HEAD_DIM),
                      kv_sb_s[:, :, 1].reshape(n_s, t_s * N_HEADS_SB, HEAD_DIM),
                      jnp.tile(g_sb[l].reshape(N_HEADS_SB, HEAD_DIM), (t_s, 1)), cache_kv_sb[l])
    o_mb = _moba_sample(page_table, qmb.reshape(n_s, t_s * N_HEADS_MB, HEAD_DIM),
                        kv_mb_s[:, :, 0].reshape(n_s, t_s * N_HEADS_MB, HEAD_DIM),
                        kv_mb_s[:, :, 1].reshape(n_s, t_s * N_HEADS_MB, HEAD_DIM),
                        jnp.tile(g_mb[l].reshape(N_HEADS_MB, HEAD_DIM), (t_s, 1)), cache_kv_mb[l])
    h1, h1b = _out_proj(o_sb.reshape(rows_s, D_SB), o_mb.reshape(rows_s, D_MB), xs, ln_g, ln_b, w_out_b,
                        _row(ln1_g[l]), _row(ln1_b[l]), rows_s, alpha)
    sc = state_conv[l].astype(F32)
    assert t_s >= CONV_WIDTH - 1 == 2
    prev1 = jnp.pad(sc[:, 1:2], ((0, 0), (0, t_s - 1), (0, 0))).reshape(rows_s, f)
    prev2 = jnp.pad(sc, ((0, 0), (0, t_s - 2), (0, 0))).reshape(rows_s, f)
    y_s, g_all = _conv_ffn(h1, h1b, wg_b, wu_b, wd_b, *ffn_consts, rows_s, rows_s, tf, t_s, rows_s, alpha,
                           state=(prev1, prev2))
    conv_s = g_all.reshape(n_s, t_s, f)[:, t_s - (CONV_WIDTH - 1):]

    return (y_p.reshape(n_p, t_p, d), y_s.reshape(n_s, t_s, d), kv_sb_p[None], kv_mb_p[None], conv_p[None],
            kv_sb_s[None], kv_mb_s[None], conv_s[None])
```
